```python
import math
import jax
import jax.numpy as jnp
from jax import lax
import numpy as np

D_MODEL = 1024
BATCH = 16
SEQ = 4096
DEPTH = 4

CHUNK = 64
N_EVEN = (DEPTH + 1) // 2
N_ODD = DEPTH // 2
D_FF = 2816
NORM_EPS = 1e-6

RW_HEADS = 8
RW_HD = 64
RW_W = RW_HEADS * RW_HD
W_LORA = 32
A_LORA = 32
V_LORA = 32
G_LORA = 96
RW_COLS = 3 * RW_W + W_LORA + A_LORA + G_LORA
GN_EPS = 64e-5

SB_HEADS = 8
SB_HD = 64
SB_W = SB_HEADS * SB_HD
SB_BLOCK = 128

EVEN_IN = RW_COLS + 3 * SB_W
EVEN_MIX = RW_W + SB_W

S5_GROUP = 16
S5_G = 16
S5_W = S5_G * S5_GROUP
S5_P = 64
DT_MIN = 1e-3
DT_MAX = 1e-1

HG_HEADS = 6
HG_DK = 128
HG_DV = 128
HG_KW = HG_HEADS * HG_DK
HG_VW = HG_HEADS * HG_DV
HG_BLOCK = CHUNK // 4

ODD_IN = S5_W + 2 * HG_KW + 2 * HG_VW
ODD_MIX = S5_W + HG_VW

kernel_name = "hybrid_rwkv7_stickbreak_s5_hgrn2_macaron"


def rmsnorm(x, gain):
    xf = x.astype(jnp.float32)
    y = xf * lax.rsqrt(jnp.mean(xf * xf, axis=-1, keepdims=True) + NORM_EPS)
    return (y * gain.astype(jnp.float32)).astype(x.dtype)


def head_rmsnorm(t, gain):
    tf = t.astype(jnp.float32)
    return tf * lax.rsqrt(jnp.mean(tf * tf, axis=-1, keepdims=True) + NORM_EPS) * gain.astype(jnp.float32)


def swiglu_ffn(h, w13, w2):
    gate, up = jnp.split(h @ w13, 2, axis=-1)
    return (jax.nn.silu(gate) * up) @ w2


def token_shift(p):
    return jnp.pad(p, ((0, 0), (1, 0), (0, 0)))[:, :-1]


def rwkv7_recurrence(r, w, k, v, a_vec, b_vec):
    bsz, _, heads, n = r.shape

    def step(state, inp):
        r_t, w_t, k_t, v_t, a_t, b_t = inp
        sa = jnp.einsum('bhvk,bhk->bhv', state, a_t)
        state = (state * w_t[:, :, None, :] + sa[..., None] * b_t[:, :, None, :]
                 + v_t[..., None] * k_t[:, :, None, :])
        return state, jnp.einsum('bhvk,bhk->bhv', state, r_t)

    xs = tuple(jnp.moveaxis(t, 1, 0) for t in (r, w, k, v, a_vec, b_vec))
    s0 = jnp.zeros((bsz, heads, n, n), jnp.float32)
    _, ys = lax.scan(step, s0, xs)
    return jnp.moveaxis(ys, 0, 1)


def stick_breaking_attention(q, k, v):
    seq = q.shape[2]
    scale = q.shape[-1] ** -0.5
    outs = []
    for start in range(0, seq, SB_BLOCK):
        end = start + SB_BLOCK
        z = jnp.einsum('bhqd,bhkd->bhqk', q[:, :, start:end], k[:, :, :end]) * scale
        t_pos = start + jnp.arange(SB_BLOCK)[:, None]
        s_pos = jnp.arange(end)[None, :]
        before = s_pos < t_pos
        log_keep = jnp.where(before, jax.nn.log_sigmoid(-z), 0.0)
        log_stick = lax.cumsum(log_keep, axis=3, reverse=True) - log_keep
        weight = jnp.where(before, jnp.exp(jax.nn.log_sigmoid(z) + log_stick), 0.0)
        outs.append(jnp.einsum('bhqk,bhkd->bhqd', weight, v[:, :, :end]))
    return jnp.concatenate(outs, axis=2)


def _linear_combine(left, right):
    a_l, b_l = left
    a_r, b_r = right
    return a_r * a_l, a_r * b_l + b_r


def s5_layer(u, a_re, a_im, b_re, b_im, c_re, c_im, d, log_dt, w_glu, b_glu):
    bsz, seq, _ = u.shape
    f32 = jnp.float32
    ug = u.reshape(bsz, seq, S5_G, S5_GROUP)
    lam = lax.complex(jnp.minimum(a_re.astype(f32), -1e-4), a_im.astype(f32))
    dt = jnp.exp(log_dt.astype(f32))[:, None]
    lam_bar = jnp.exp(lam * dt)
    b_bar = ((lam_bar - 1.0) / lam)[..., None] * lax.complex(b_re.astype(f32), b_im.astype(f32))
    bu = jnp.einsum('gpc,bsgc->bsgp', b_bar, ug.astype(jnp.complex64))
    lam_seq = jnp.broadcast_to(lam_bar, (1, seq) + lam_bar.shape)
    _, states = lax.associative_scan(_linear_combine, (lam_seq, bu), axis=1)
    c = lax.complex(c_re.astype(f32), c_im.astype(f32))
    y = jnp.einsum('gcp,bsgp->bsgc', c, states).real.reshape(bsz, seq, S5_W) + d * u
    y = jax.nn.gelu(y)
    return y * jax.nn.sigmoid(y @ w_glu + b_glu)


def hgrn2_chunkwise(q, f_logit, i_val, lb):
    bsz, seq, _ = q.shape
    n_chunks = seq // HG_BLOCK
    log_f = jnp.logaddexp(jnp.log(lb), jnp.log1p(-lb) + jax.nn.log_sigmoid(f_logit))
    k = (1.0 - lb) * jax.nn.sigmoid(-f_logit)

    def chunks(t, dim):
        return t.reshape(bsz, n_chunks, HG_BLOCK, HG_HEADS, dim).transpose(1, 0, 3, 2, 4)

    qc, kc, gc, vc = chunks(q, HG_DK), chunks(k, HG_DK), chunks(log_f, HG_DK), chunks(i_val, HG_DV)
    g_cum = jnp.cumsum(gc, axis=3)
    g_last = g_cum[:, :, :, -1:, :]
    q_in = qc * jnp.exp(g_cum)
    k_in = kc * jnp.exp(-g_cum)
    k_end = kc * jnp.exp(g_last - g_cum)
    decay_end = jnp.exp(g_last[:, :, :, 0, :])
    causal = jnp.tril(jnp.ones((HG_BLOCK, HG_BLOCK), dtype=bool))

    def step(state, xs):
        q_c, k_c, ke_c, v_c, d_c = xs
        att = jnp.where(causal, jnp.einsum('bhtk,bhsk->bhts', q_c, k_c), 0.0)
        o = jnp.einsum('bhtk,bhkv->bhtv', q_c, state) + jnp.einsum('bhts,bhsv->bhtv', att, v_c)
        state = d_c[..., None] * state + jnp.einsum('bhsk,bhsv->bhkv', ke_c, v_c)
        return state, o

    s0 = jnp.zeros((bsz, HG_HEADS, HG_DK, HG_DV), jnp.float32)
    _, o = lax.scan(step, s0, (q_in, k_in, k_end, vc, decay_end))
    return o.transpose(1, 0, 3, 2, 4).reshape(bsz, seq, HG_HEADS, HG_DV)


def even_mixer(h, w_in, w_out, mu, w0, w2, a0, a2, g2, k_k, k_a, r_k, ln_w, ln_b,
               q_gain, k_gain, v_first, v_mix):
    bsz, seq, _ = h.shape
    proj = (h @ w_in).astype(jnp.float32)
    rw, sb = proj[..., :RW_COLS], proj[..., RW_COLS:]

    rw = rw + mu * (token_shift(rw) - rw)
    cuts = [RW_W, 2 * RW_W, 3 * RW_W, 3 * RW_W + W_LORA, 3 * RW_W + W_LORA + A_LORA]
    r, k, v, w_d, a_d, g_d = jnp.split(rw, cuts, axis=-1)
    log_w = -jax.nn.softplus(-(w0 + jnp.tanh(w_d) @ w2)) - 0.5
    decay = jnp.exp(-jnp.exp(log_w))
    a = jax.nn.sigmoid(a0 + a_d @ a2)
    g = jax.nn.sigmoid(g_d) @ g2
    if v_mix is not None:
        v0, v1, v2 = v_mix
        v = v + (v_first - v) * jax.nn.sigmoid(v0 + (h @ v1) @ v2)

    def heads(t):
        return t.reshape(bsz, seq, RW_HEADS, RW_HD)

    kk = heads(k * k_k)
    kk = kk * lax.rsqrt(jnp.maximum(jnp.sum(kk * kk, axis=-1, keepdims=True), 1e-24))
    k = k * (1.0 + (a - 1.0) * k_a)
    rh, kh, vh = heads(r), heads(k), heads(v)
    y = rwkv7_recurrence(rh, heads(decay), kh, vh, -kk, kk * heads(a))
    mean = jnp.mean(y, axis=-1, keepdims=True)
    var = jnp.mean(jnp.square(y - mean), axis=-1, keepdims=True)
    y = ((y - mean) * lax.rsqrt(var + GN_EPS)).reshape(bsz, seq, RW_W) * ln_w + ln_b
    bonus = jnp.sum(rh * kh * r_k.reshape(RW_HEADS, RW_HD), axis=-1, keepdims=True) * vh
    y_rw = (y + bonus.reshape(bsz, seq, RW_W)) * g

    def to_bhsd(t):
        return t.reshape(bsz, seq, SB_HEADS, SB_HD).transpose(0, 2, 1, 3)

    q_sb, k_sb, v_sb = jnp.split(sb, 3, axis=-1)
    y_sb = stick_breaking_attention(head_rmsnorm(to_bhsd(q_sb), q_gain),
                                    head_rmsnorm(to_bhsd(k_sb), k_gain), to_bhsd(v_sb))
    y_sb = y_sb.transpose(0, 2, 1, 3).reshape(bsz, seq, SB_W)

    out = jnp.concatenate([y_rw, y_sb], axis=-1).astype(h.dtype) @ w_out
    return out, v


def odd_mixer(h, w_in, w_out, a_re, a_im, b_re, b_im, c_re, c_im, d, log_dt, w_glu, b_glu, lb, hg_gain):
    bsz, seq, _ = h.shape
    proj = (h @ w_in).astype(jnp.float32)
    cuts = [S5_W, S5_W + HG_KW, S5_W + 2 * HG_KW, S5_W + 2 * HG_KW + HG_VW]
    u, q, f_logit, i_val, gate = jnp.split(proj, cuts, axis=-1)
    y_s5 = s5_layer(u, a_re, a_im, b_re, b_im, c_re, c_im, d, log_dt, w_glu, b_glu)
    o = hgrn2_chunkwise(q, f_logit, i_val, lb.astype(jnp.float32))
    y_hg = head_rmsnorm(o, hg_gain.reshape(HG_HEADS, HG_DV)).reshape(bsz, seq, HG_VW) * jax.nn.silu(gate)
    return jnp.concatenate([y_s5, y_hg], axis=-1).astype(h.dtype) @ w_out


def setup_inputs(seed: int = 0) -> dict:
    key = jax.random.key(seed)
    keys = iter(jax.random.split(key, 64))
    f32 = jnp.float32

    def normal(shape, scale):
        return scale * jax.random.normal(next(keys), shape, f32)

    def gain(shape):
        return 1.0 + normal(shape, 0.02)

    n_vres = N_EVEN - 1
    ramp = jnp.tile(jnp.arange(RW_HD, dtype=f32) / (RW_HD - 1), RW_HEADS)
    return {
        "x": normal((BATCH, SEQ, D_MODEL), 1.0),
        "ffn1_norm": gain((DEPTH, D_MODEL)),
        "ffn1_w13": normal((DEPTH, D_MODEL, 2 * D_FF), D_MODEL ** -0.5),
        "ffn1_w2": normal((DEPTH, D_FF, D_MODEL), D_FF ** -0.5),
        "mix_norm": gain((DEPTH, D_MODEL)),
        "ffn2_norm": gain((DEPTH, D_MODEL)),
        "ffn2_w13": normal((DEPTH, D_MODEL, 2 * D_FF), D_MODEL ** -0.5),
        "ffn2_w2": normal((DEPTH, D_FF, D_MODEL), D_FF ** -0.5),
        "ev_w_in": normal((N_EVEN, D_MODEL, EVEN_IN), D_MODEL ** -0.5),
        "ev_w_out": normal((N_EVEN, EVEN_MIX, D_MODEL), EVEN_MIX ** -0.5),
        "rw_mu": jax.random.uniform(next(keys), (N_EVEN, RW_COLS), f32),
        "rw_w0": -6.0 + 7.0 * ramp + normal((N_EVEN, RW_W), 0.1),
        "rw_w2": normal((N_EVEN, W_LORA, RW_W), 0.5 * W_LORA ** -0.5),
        "rw_a0": normal((N_EVEN, RW_W), 0.1),
        "rw_a2": normal((N_EVEN, A_LORA, RW_W), 0.5 * A_LORA ** -0.5),
        "rw_g2": normal((N_EVEN, G_LORA, RW_W), G_LORA ** -0.5),
        "rw_k_k": 0.85 + normal((N_EVEN, RW_W), 0.05),
        "rw_k_a": 1.0 + normal((N_EVEN, RW_W), 0.05),
        "rw_r_k": normal((N_EVEN, RW_W), 0.1),
        "rw_ln_w": gain((N_EVEN, RW_W)),
        "rw_ln_b": normal((N_EVEN, RW_W), 0.02),
        "rw_v0": 1.0 + normal((n_vres, RW_W), 0.1),
        "rw_v1": normal((n_vres, D_MODEL, V_LORA), D_MODEL ** -0.5),
        "rw_v2": normal((n_vres, V_LORA, RW_W), V_LORA ** -0.5),
        "sb_q_gain": gain((N_EVEN, SB_HD)),
        "sb_k_gain": gain((N_EVEN, SB_HD)),
        "od_w_in": normal((N_ODD, D_MODEL, ODD_IN), D_MODEL ** -0.5),
        "od_w_out": normal((N_ODD, ODD_MIX, D_MODEL), ODD_MIX ** -0.5),
        "s5_a_re": -0.5 + normal((N_ODD, S5_G, S5_P), 0.01),
        "s5_a_im": jnp.pi * jnp.arange(S5_P, dtype=f32) + normal((N_ODD, S5_G, S5_P), 0.01),
        "s5_b_re": normal((N_ODD, S5_G, S5_P, S5_GROUP), (2 * S5_GROUP) ** -0.5),
        "s5_b_im": normal((N_ODD, S5_G, S5_P, S5_GROUP), (2 * S5_GROUP) ** -0.5),
        "s5_c_re": normal((N_ODD, S5_G, S5_GROUP, S5_P), S5_P ** -0.5),
        "s5_c_im": normal((N_ODD, S5_G, S5_GROUP, S5_P), S5_P ** -0.5),
        "s5_d": normal((N_ODD, S5_W), 1.0),
        "s5_log_dt": jax.random.uniform(next(keys), (N_ODD, S5_G), f32, math.log(DT_MIN), math.log(DT_MAX)),
        "s5_w_glu": normal((N_ODD, S5_W, S5_W), S5_W ** -0.5),
        "s5_b_glu": normal((N_ODD, S5_W), 0.02),
        "hg_lb": normal((N_ODD, HG_KW), 1.0),
        "hg_gain": gain((N_ODD, HG_VW)),
    }


def reference(x, ffn1_norm, ffn1_w13, ffn1_w2, mix_norm, ffn2_norm, ffn2_w13, ffn2_w2,
              ev_w_in, ev_w_out, rw_mu, rw_w0, rw_w2, rw_a0, rw_a2, rw_g2, rw_k_k, rw_k_a, rw_r_k,
              rw_ln_w, rw_ln_b, rw_v0, rw_v1, rw_v2, sb_q_gain, sb_k_gain,
              od_w_in, od_w_out, s5_a_re, s5_a_im, s5_b_re, s5_b_im, s5_c_re, s5_c_im, s5_d,
              s5_log_dt, s5_w_glu, s5_b_glu, hg_lb, hg_gain):
    lb_all = jnp.cumsum(jax.nn.softmax(hg_lb.astype(jnp.float32), axis=0), axis=0)
    lb_all = lb_all - lb_all[0]
    v_first = None
    for layer in range(DEPTH):
        x = x + 0.5 * swiglu_ffn(rmsnorm(x, ffn1_norm[layer]), ffn1_w13[layer], ffn1_w2[layer])
        h = rmsnorm(x, mix_norm[layer])
        if layer % 2 == 0:
            e = layer // 2
            v_mix = None if e == 0 else (rw_v0[e - 1], rw_v1[e - 1], rw_v2[e - 1])
            mixed, v = even_mixer(h, ev_w_in[e], ev_w_out[e], rw_mu[e], rw_w0[e], rw_w2[e], rw_a0[e],
                                  rw_a2[e], rw_g2[e], rw_k_k[e], rw_k_a[e], rw_r_k[e], rw_ln_w[e],
                                  rw_ln_b[e], sb_q_gain[e], sb_k_gain[e], v_first, v_mix)
            if e == 0:
                v_first = v
        else:
            o = layer // 2
            mixed = odd_mixer(h, od_w_in[o], od_w_out[o], s5_a_re[o], s5_a_im[o], s5_b_re[o], s5_b_im[o],
                              s5_c_re[o], s5_c_im[o], s5_d[o], s5_log_dt[o], s5_w_glu[o], s5_b_glu[o],
                              lb_all[o], hg_gain[o])
        x = x + mixed.astype(x.dtype)
        x = x + 0.5 * swiglu_ffn(rmsnorm(x, ffn2_norm[layer]), ffn2_w13[layer], ffn2_w2[layer])
    return x
```

```python
import functools

import jax
import jax.numpy as jnp
from jax import lax
from jax.experimental import pallas as pl
from jax.experimental.pallas import tpu as pltpu

F32 = jnp.float32
BF16 = jnp.bfloat16

NORM_EPS = 1e-6
GN_EPS = 64e-5
LANES = 128
RW_HEADS, RW_HD, RW_W = 8, 64, 512
W_LORA, A_LORA, V_LORA, G_LORA = 32, 32, 32, 96
RW_SMALL = 256
RW_PAD = 3 * RW_W + RW_SMALL
SB_W, SB_HD = 512, 64
EVEN_PAD = RW_PAD + 3 * SB_W
S5_G, S5_GROUP, S5_W, S5_P = 16, 16, 256, 64
HG_HEADS, HG_DK, HG_KW, HG_VW, HG_BLOCK = 6, 128, 768, 768, 16
ODD_IN = S5_W + 2 * HG_KW + 2 * HG_VW
VMEM_LIMIT = 48 * 1024 * 1024

RW_CHUNK = 64
S5_CHUNK = 64
SB_TQ = 128
HG_TS = 128


def _cparams(sem):
    return pltpu.CompilerParams(dimension_semantics=sem, vmem_limit_bytes=VMEM_LIMIT)


def _dot(a, b):
    return jnp.dot(a.astype(BF16), b.astype(BF16), preferred_element_type=F32)


def _dot_nt(a, b):
    return lax.dot_general(a.astype(BF16), b.astype(BF16), (((1,), (1,)), ((), ())),
                           preferred_element_type=F32)


def _dot_tn(a, b):
    return lax.dot_general(a.astype(BF16), b.astype(BF16), (((0,), (0,)), ((), ())),
                           preferred_element_type=F32)


def _dot_hi(a, b):
    return jnp.dot(a, b, preferred_element_type=F32, precision=lax.Precision.HIGHEST)


def _split(x):
    hi = x.astype(BF16)
    lo = (x - hi.astype(F32)).astype(BF16)
    return hi, lo


def _dot_sel_right(x, e):
    hi, lo = _split(x)
    return (jnp.dot(hi, e, preferred_element_type=F32) + jnp.dot(lo, e, preferred_element_type=F32))


def _dot_sel_left(e, x):
    hi, lo = _split(x)
    return (jnp.dot(e, hi, preferred_element_type=F32) + jnp.dot(e, lo, preferred_element_type=F32))


def _softplus(x):
    return jnp.maximum(x, 0.0) + jnp.log1p(jnp.exp(-jnp.abs(x)))


def _sigmoid(x):
    return jax.nn.sigmoid(x)


def _iota2(shape, dim):
    return lax.broadcasted_iota(jnp.int32, shape, dim)


def _group_ones(n, group):
    return (_iota2((n, n), 0) // group == _iota2((n, n), 1) // group).astype(BF16)


def _rms_rows(x, gain):
    return x * lax.rsqrt(jnp.mean(x * x, axis=-1, keepdims=True) + NORM_EPS) * gain


def _ffn_kernel(x_ref, g_ref, w1_ref, w3_ref, w2_ref, o_ref, h_scr, acc_scr):
    j = pl.program_id(1)

    @pl.when(j == 0)
    def _():
        h_scr[...] = _rms_rows(x_ref[...], g_ref[...]).astype(BF16)
        acc_scr[...] = jnp.zeros_like(acc_scr)

    h = h_scr[...]
    gate = jnp.dot(h, w1_ref[...], preferred_element_type=F32)
    up = jnp.dot(h, w3_ref[...], preferred_element_type=F32)
    act = (gate * _sigmoid(gate) * up).astype(BF16)
    acc_scr[...] += jnp.dot(act, w2_ref[...], preferred_element_type=F32)

    @pl.when(j == pl.num_programs(1) - 1)
    def _():
        o_ref[...] = x_ref[...] + 0.5 * acc_scr[...]


def _ffn(x2, gain, w13, w2):
    n, d = x2.shape
    dff = w2.shape[0]
    tm = min(1024, n)
    tf = 256
    nf = dff // tf
    return pl.pallas_call(
        _ffn_kernel,
        out_shape=jax.ShapeDtypeStruct((n, d), F32),
        grid=(n // tm, nf),
        in_specs=[
            pl.BlockSpec((tm, d), lambda i, j: (i, 0)),
            pl.BlockSpec((1, d), lambda i, j: (0, 0)),
            pl.BlockSpec((d, tf), lambda i, j: (0, j)),
            pl.BlockSpec((d, tf), lambda i, j: (0, j + nf)),
            pl.BlockSpec((tf, d), lambda i, j: (j, 0)),
        ],
        out_specs=pl.BlockSpec((tm, d), lambda i, j: (i, 0)),
        scratch_shapes=[pltpu.VMEM((tm, d), BF16), pltpu.VMEM((tm, d), F32)],
        compiler_params=_cparams(("parallel", "arbitrary")),
        name="ffn",
    )(x2, gain.reshape(1, d), w13.astype(BF16), w13.astype(BF16), w2.astype(BF16))


def _proj_kernel(x_ref, g_ref, w_ref, o_ref, h_scr):
    @pl.when(pl.program_id(1) == 0)
    def _():
        h_scr[...] = _rms_rows(x_ref[...], g_ref[...]).astype(BF16)

    o_ref[...] = jnp.dot(h_scr[...], w_ref[...], preferred_element_type=F32)


def _proj(x2, gain, w):
    n, d = x2.shape
    c = w.shape[1]
    tm = min(1024, n)
    tn = c // 2
    return pl.pallas_call(
        _proj_kernel,
        out_shape=jax.ShapeDtypeStruct((n, c), F32),
        grid=(n // tm, c // tn),
        in_specs=[
            pl.BlockSpec((tm, d), lambda i, j: (i, 0)),
            pl.BlockSpec((1, d), lambda i, j: (0, 0)),
            pl.BlockSpec((d, tn), lambda i, j: (0, j)),
        ],
        out_specs=pl.BlockSpec((tm, tn), lambda i, j: (i, j)),
        scratch_shapes=[pltpu.VMEM((tm, d), BF16)],
        compiler_params=_cparams(("parallel", "arbitrary")),
        name="mixer_in_proj",
    )(x2, gain.reshape(1, d), w.astype(BF16))


def _outproj_kernel(x_ref, a1_ref, a2_ref, w1_ref, w2_ref, o_ref):
    o_ref[...] = (x_ref[...]
                  + jnp.dot(a1_ref[...], w1_ref[...], preferred_element_type=F32)
                  + jnp.dot(a2_ref[...], w2_ref[...], preferred_element_type=F32))


def _outproj(x2, a1, a2, w_out):
    n, d = x2.shape
    k1, k2 = a1.shape[1], a2.shape[1]
    tm = min(1024, n)
    w = w_out.astype(BF16)
    return pl.pallas_call(
        _outproj_kernel,
        out_shape=jax.ShapeDtypeStruct((n, d), F32),
        grid=(n // tm,),
        in_specs=[
            pl.BlockSpec((tm, d), lambda i: (i, 0)),
            pl.BlockSpec((tm, k1), lambda i: (i, 0)),
            pl.BlockSpec((tm, k2), lambda i: (i, 0)),
            pl.BlockSpec((k1, d), lambda i: (0, 0)),
            pl.BlockSpec((k2, d), lambda i: (0, 0)),
        ],
        out_specs=pl.BlockSpec((tm, d), lambda i: (i, 0)),
        compiler_params=_cparams(("parallel",)),
        name="mixer_out_proj",
    )(x2, a1, a2, w[:k1], w[k1:])


def _rwkv_prep_kernel(*refs, has_vres):
    if has_vres:
        (p_ref, mu_ref, w0_ref, w2_ref, a0_ref, a2_ref, g2_ref, kk_ref, ka_ref, v0_ref, v2_ref, vf_ref,
         r_o, k_o, v_o, lw_o, kk_o, a_o, g_o, carry) = refs
    else:
        (p_ref, mu_ref, w0_ref, w2_ref, a0_ref, a2_ref, g2_ref, kk_ref, ka_ref,
         r_o, k_o, v_o, lw_o, kk_o, a_o, g_o, carry) = refs
    ts = p_ref.shape[1]

    @pl.when(pl.program_id(1) == 0)
    def _():
        carry[...] = jnp.zeros_like(carry)

    x = p_ref[0]
    prev = pltpu.roll(x, 1, axis=0)
    prev = jnp.where(_iota2((ts, 1), 0) == 0, carry[...], prev)
    carry[...] = x[ts - 1:ts, :]
    rw = x + mu_ref[...] * (prev - x)

    r = rw[:, 0:RW_W]
    k = rw[:, RW_W:2 * RW_W]
    v = rw[:, 2 * RW_W:3 * RW_W]
    small = rw[:, 3 * RW_W:RW_PAD]

    log_w = -_softplus(-(w0_ref[...] + _dot(jnp.tanh(small), w2_ref[...]))) - 0.5
    a = _sigmoid(a0_ref[...] + _dot(small, a2_ref[...]))
    g = _dot(_sigmoid(small), g2_ref[...])
    if has_vres:
        v = v + (vf_ref[0] - v) * _sigmoid(v0_ref[...] + _dot(small, v2_ref[...]))

    kk = k * kk_ref[...]
    ss = _dot_sel_right(kk * kk, _group_ones(RW_W, RW_HD))
    kk = kk * lax.rsqrt(jnp.maximum(ss, 1e-24))

    r_o[0] = r
    k_o[0] = k * (1.0 + (a - 1.0) * ka_ref[...])
    v_o[0] = v
    lw_o[0] = -jnp.exp(log_w)
    kk_o[0] = kk
    a_o[0] = a
    g_o[0] = g


def _rwkv_prep(proj3, mu, w0, w2p, a0, a2p, g2p, k_k, k_a, vres):
    b, s, _ = proj3.shape
    ts = min(512, s)
    has_vres = vres is not None
    row = lambda t: t.reshape(1, -1)
    vec = lambda n: pl.BlockSpec((1, n), lambda i, j: (0, 0))
    lora = pl.BlockSpec((RW_SMALL, RW_W), lambda i, j: (0, 0))
    seq = pl.BlockSpec((1, ts, RW_W), lambda i, j: (i, j, 0))
    args = [proj3, row(mu), row(w0), w2p, row(a0), a2p, g2p, row(k_k), row(k_a)]
    specs = [pl.BlockSpec((1, ts, RW_PAD), lambda i, j: (i, j, 0)), vec(RW_PAD), vec(RW_W), lora,
             vec(RW_W), lora, lora, vec(RW_W), vec(RW_W)]
    if has_vres:
        v0, v2p, v_first = vres
        args += [row(v0), v2p, v_first]
        specs += [vec(RW_W), lora, seq]
    out = jax.ShapeDtypeStruct((b, s, RW_W), F32)
    return pl.pallas_call(
        functools.partial(_rwkv_prep_kernel, has_vres=has_vres),
        out_shape=[out] * 7,
        grid=(b, s // ts),
        in_specs=specs,
        out_specs=[seq] * 7,
        scratch_shapes=[pltpu.VMEM((1, RW_PAD), F32)],
        compiler_params=_cparams(("parallel", "arbitrary")),
        name="rwkv_prep",
    )(*args)


def _unit_lower_inverse(n_strict):
    l = n_strict.shape[0]
    row, col = _iota2((l, l), 0), _iota2((l, l), 1)
    eye = (row == col).astype(F32)
    same = lambda blk: (row // blk) == (col // blk)
    n1 = jnp.where(same(8), n_strict, 0.0)
    n2 = _dot_hi(n1, n1)
    n4 = _dot_hi(n2, n2)
    p1 = eye + n1 + n2 + _dot_hi(n1, n2)
    t = p1 + _dot_hi(p1, n4)
    blk = 8
    while blk < l:
        nl = jnp.where(same(2 * blk) & jnp.logical_not(same(blk)), n_strict, 0.0)
        t = t + _dot_hi(_dot_hi(t, nl), t)
        blk *= 2
    return t


def _rwkv_chunk_kernel(r_ref, k_ref, v_ref, lw_ref, kk_ref, a_ref, g_ref, rk_ref, lnw_ref, lnb_ref,
                       o_ref, h_scr):
    l = r_ref.shape[1]

    @pl.when(pl.program_id(1) == 0)
    def _():
        h_scr[...] = jnp.zeros_like(h_scr)

    row, col = _iota2((l, l), 0), _iota2((l, l), 1)
    tri_incl = (col <= row).astype(BF16)
    strict, incl = col < row, col <= row
    m0 = _iota2((1, LANES), 1) < RW_HD
    m0_wide = (_iota2((1, 2 * LANES), 1) % LANES) < RW_HD
    r2, c2 = _iota2((LANES, LANES), 0), _iota2((LANES, LANES), 1)
    same_head = (r2 // RW_HD) == (c2 // RW_HD)
    eye2 = r2 == c2
    head_ones = _group_ones(LANES, RW_HD)

    for p in range(RW_W // LANES):
        sl = slice(p * LANES, (p + 1) * LANES)
        r, k, v = r_ref[0, :, sl], k_ref[0, :, sl], v_ref[0, :, sl]
        lw, kk, a = lw_ref[0, :, sl], kk_ref[0, :, sl], a_ref[0, :, sl]

        gc = _dot_sel_left(tri_incl, lw)
        g_last = gc[l - 1:l, :]
        e_pos, e_neg = jnp.exp(gc), jnp.exp(-gc)
        e_end = jnp.exp(g_last - gc)
        bv = kk * a
        a_t = -kk * jnp.exp(gc - lw)
        b_t, k_t, r_t = bv * e_neg, k * e_neg, r * e_pos
        b_e, k_e = bv * e_end, k * e_end

        zero = jnp.zeros_like(a_t)
        lhs = jnp.concatenate([jnp.where(m0, a_t, zero), jnp.where(m0, zero, a_t),
                               jnp.where(m0, r_t, zero), jnp.where(m0, zero, r_t)], axis=0)
        pb = _dot_nt(lhs, b_t)
        pk = _dot_nt(lhs, k_t)

        t_inv, x_h, r_b, r_k = [], [], [], []
        for h in range(2):
            t_inv.append(_unit_lower_inverse(jnp.where(strict, pb[h * l:(h + 1) * l], 0.0)))
            x_h.append(_dot(jnp.where(strict, pk[h * l:(h + 1) * l], 0.0), v))
            r_b.append(jnp.where(incl, pb[(2 + h) * l:(3 + h) * l], 0.0))
            r_k.append(jnp.where(incl, pk[(2 + h) * l:(3 + h) * l], 0.0))
        x = jnp.where(m0, x_h[0], x_h[1])
        ax = jnp.concatenate([a_t, x], axis=1)
        wu = jnp.where(m0_wide, _dot_hi(t_inv[0], ax), _dot_hi(t_inv[1], ax))
        w_mat, u0 = wu[:, :LANES], wu[:, LANES:]

        rwu = [_dot(r_b[h], wu) for h in range(2)]
        rv = [_dot(r_k[h], v) for h in range(2)]
        q_p = r_t + jnp.where(m0, rwu[0][:, :LANES], rwu[1][:, :LANES])
        y0 = jnp.where(m0, rwu[0][:, LANES:] + rv[0], rwu[1][:, LANES:] + rv[1])

        mn = _dot_tn(b_e, wu)
        kv = _dot_tn(k_e, v)
        m_mat = (jnp.where(same_head, mn[:, :LANES], 0.0)
                 + jnp.where(eye2, jnp.broadcast_to(jnp.exp(g_last), (LANES, LANES)), 0.0))
        n_mat = jnp.where(same_head, mn[:, LANES:] + kv, 0.0)

        h_prev = h_scr[p]
        y = _dot(q_p, h_prev) + y0
        h_scr[p] = _dot_hi(m_mat, h_prev) + n_mat

        mean = _dot_sel_right(y, head_ones) * (1.0 / RW_HD)
        yc = y - mean
        var = _dot_sel_right(yc * yc, head_ones) * (1.0 / RW_HD)
        yn = yc * lax.rsqrt(var + GN_EPS) * lnw_ref[:, sl] + lnb_ref[:, sl]
        bonus = _dot_sel_right(r * k * rk_ref[:, sl], head_ones) * v
        o_ref[0, :, sl] = ((yn + bonus) * g_ref[0, :, sl]).astype(o_ref.dtype)


def _rwkv_chunk(r, k, v, lw, kk, a, g, r_k, ln_w, ln_b):
    b, s, _ = r.shape
    l = RW_CHUNK
    seq = pl.BlockSpec((1, l, RW_W), lambda i, j: (i, j, 0))
    vec = pl.BlockSpec((1, RW_W), lambda i, j: (0, 0))
    return pl.pallas_call(
        _rwkv_chunk_kernel,
        out_shape=jax.ShapeDtypeStruct((b, s, RW_W), BF16),
        grid=(b, s // l),
        in_specs=[seq] * 7 + [vec] * 3,
        out_specs=seq,
        scratch_shapes=[pltpu.VMEM((RW_W // LANES, LANES, LANES), F32)],
        compiler_params=_cparams(("parallel", "arbitrary")),
        name="rwkv_chunk",
    )(r, k, v, lw, kk, a, g, r_k.reshape(1, -1), ln_w.reshape(1, -1), ln_b.reshape(1, -1))


def _pair_rmsnorm(t, gain_row, ones):
    ms = _dot_sel_right(t * t, ones) * (1.0 / SB_HD)
    return t * lax.rsqrt(ms + NORM_EPS) * gain_row


def _sb_kernel(q_ref, k_ref, v_ref, qg_ref, kg_ref, o_ref, kn_scr, c_scr, acc_scr):
    tq = q_ref.shape[1]
    tk = tq
    qi = pl.program_id(2)
    ones = _group_ones(LANES, SB_HD)

    @pl.when(qi == 0)
    def _():
        kn_scr[...] = _pair_rmsnorm(k_ref[0], kg_ref[...], ones).astype(BF16)

    q = _pair_rmsnorm(q_ref[0], qg_ref[...], ones) * (SB_HD ** -0.5)
    m0 = _iota2((1, LANES), 1) < SB_HD
    qh = [jnp.where(m0, q, 0.0).astype(BF16), jnp.where(m0, 0.0, q).astype(BF16)]
    tri_after = (_iota2((tk, tk), 0) > _iota2((tk, tk), 1)).astype(BF16)
    all_ones = jnp.ones((tk, LANES), BF16)
    before = _iota2((tq, tk), 1) < _iota2((tq, tk), 0)

    c_scr[...] = jnp.zeros_like(c_scr)
    acc_scr[...] = jnp.zeros_like(acc_scr)

    def block(j, mask):
        kb = kn_scr[pl.ds(j * tk, tk), :]
        vb = v_ref[0, pl.ds(j * tk, tk), :].astype(BF16)
        pv = []
        for h in range(2):
            z = lax.dot_general(qh[h], kb, (((1,), (1,)), ((), ())), preferred_element_type=F32)
            log_keep = -_softplus(z)
            if mask is not None:
                log_keep = jnp.where(mask, log_keep, 0.0)
            hi, lo = _split(log_keep)
            after = (jnp.dot(hi, tri_after, preferred_element_type=F32)
                     + jnp.dot(lo, tri_after, preferred_element_type=F32))
            total = (jnp.dot(hi, all_ones, preferred_element_type=F32)
                     + jnp.dot(lo, all_ones, preferred_element_type=F32))
            c = c_scr[h]
            wgt = jnp.exp(z + log_keep + after + c)
            if mask is not None:
                wgt = jnp.where(mask, wgt, 0.0)
            c_scr[h] = c + total
            pv.append(jnp.dot(wgt.astype(BF16), vb, preferred_element_type=F32))
        acc_scr[...] += jnp.where(m0, pv[0], pv[1])

    block(qi, before)

    def body(it, carry):
        block(qi - 1 - it, None)
        return carry

    lax.fori_loop(0, qi, body, 0)
    o_ref[0] = acc_scr[...].astype(o_ref.dtype)


def _sb_attention(proj3, q_gain, k_gain, col0):
    b, s, _ = proj3.shape
    tq = min(SB_TQ, s)
    npair = SB_W // LANES
    gain2 = lambda g: jnp.tile(g, 2).reshape(1, LANES)
    return pl.pallas_call(
        _sb_kernel,
        out_shape=jax.ShapeDtypeStruct((b, s, SB_W), BF16),
        grid=(b, npair, s // tq),
        in_specs=[
            pl.BlockSpec((1, tq, LANES), lambda i, p, j: (i, j, col0 + p)),
            pl.BlockSpec((1, s, LANES), lambda i, p, j: (i, 0, col0 + npair + p)),
            pl.BlockSpec((1, s, LANES), lambda i, p, j: (i, 0, col0 + 2 * npair + p)),
            pl.BlockSpec((1, LANES), lambda i, p, j: (0, 0)),
            pl.BlockSpec((1, LANES), lambda i, p, j: (0, 0)),
        ],
        out_specs=pl.BlockSpec((1, tq, LANES), lambda i, p, j: (i, j, p)),
        scratch_shapes=[pltpu.VMEM((s, LANES), BF16), pltpu.VMEM((2, tq, LANES), F32),
                        pltpu.VMEM((tq, LANES), F32)],
        compiler_params=_cparams(("parallel", "parallel", "arbitrary")),
        name="stick_breaking",
    )(proj3, proj3, proj3, gain2(q_gain), gain2(k_gain))


def _s5_tables(a_re, a_im, b_re, b_im, c_re, c_im, log_dt, l):
    lam = lax.complex(jnp.minimum(a_re.astype(F32), -1e-4), a_im.astype(F32))
    lam_dt = lam * jnp.exp(log_dt.astype(F32))[:, None]
    lam_bar = jnp.exp(lam_dt)
    b_bar = ((lam_bar - 1.0) / lam)[..., None] * lax.complex(b_re.astype(F32), b_im.astype(F32))
    c = lax.complex(c_re.astype(F32), c_im.astype(F32))
    steps = jnp.arange(l + 1, dtype=F32)
    powers = jnp.exp(lam_dt[:, None, :] * steps[None, :, None])
    g, p = lam.shape
    kern = jnp.einsum('gcp,gjp,gpd->gjcd', c, powers[:, :l], b_bar).real
    lag = jnp.arange(l)[None, :] - jnp.arange(l)[:, None]
    toep = jnp.where((lag >= 0)[None, :, :, None, None], kern[:, jnp.clip(lag, 0, l - 1)], 0.0)
    toep = toep.transpose(0, 1, 4, 2, 3).reshape(g, l * S5_GROUP, l * S5_GROUP)
    b_end = powers[:, :l][:, ::-1, :, None] * b_bar[:, None, :, :]
    b_end = b_end.transpose(0, 1, 3, 2).reshape(g, l * S5_GROUP, p)
    b_end = jnp.concatenate([b_end.real, b_end.imag], axis=-1)
    c_pow = c[:, None, :, :] * powers[:, 1:, None, :]
    c_pow = jnp.concatenate([c_pow.real, -c_pow.imag], axis=-1)
    c_pow = c_pow.transpose(0, 3, 1, 2).reshape(g, 2 * p, l * S5_GROUP)
    lam_l = powers[:, l]
    mul_same = jnp.concatenate([lam_l.real, lam_l.real], axis=-1).reshape(g, 1, 2 * p)
    mul_swap = jnp.concatenate([-lam_l.imag, lam_l.imag], axis=-1).reshape(g, 1, 2 * p)
    return toep.astype(BF16), b_end.astype(BF16), c_pow.astype(BF16), mul_same, mul_swap


def _s5_kernel(u_ref, toep_ref, bend_ref, cpow_ref, same_ref, swap_ref, y_ref, e_scr, x_scr, *, bsz):
    u = u_ref[0].astype(BF16)
    e_scr[...] = jnp.dot(u, bend_ref[0], preferred_element_type=F32)
    n_chunks = u.shape[0] // bsz
    same, swap = same_ref[0], swap_ref[0]

    def body(c, x):
        rows = pl.ds(pl.multiple_of(c * bsz, bsz), bsz)
        x_scr[rows, :] = x
        return x * same + pltpu.roll(x, S5_P, axis=1) * swap + e_scr[rows, :]

    lax.fori_loop(0, n_chunks, body, jnp.zeros((bsz, 2 * S5_P), F32))
    y_ref[0] = (jnp.dot(u, toep_ref[0], preferred_element_type=F32)
                + jnp.dot(x_scr[...].astype(BF16), cpow_ref[0], preferred_element_type=F32))


def _s5_ssm(u, tables):
    b, s, _ = u.shape
    l = min(S5_CHUNK, s)
    nc = s // l
    toep, b_end, c_pow, mul_same, mul_swap = tables
    width = l * S5_GROUP
    ug = u.reshape(b, nc, l, S5_G, S5_GROUP).transpose(3, 1, 0, 2, 4).reshape(S5_G, nc * b, width)
    grp = lambda shape: pl.BlockSpec((1,) + shape, lambda g: (g, 0, 0))
    y = pl.pallas_call(
        functools.partial(_s5_kernel, bsz=b),
        out_shape=jax.ShapeDtypeStruct((S5_G, nc * b, width), F32),
        grid=(S5_G,),
        in_specs=[grp((nc * b, width)), grp((width, width)), grp((width, 2 * S5_P)),
                  grp((2 * S5_P, width)), grp((1, 2 * S5_P)), grp((1, 2 * S5_P))],
        out_specs=grp((nc * b, width)),
        scratch_shapes=[pltpu.VMEM((nc * b, 2 * S5_P), F32), pltpu.VMEM((nc * b, 2 * S5_P), F32)],
        compiler_params=_cparams(("parallel",)),
        name="s5_ssm",
    )(ug, toep, b_end, c_pow, mul_same, mul_swap)
    return y.reshape(S5_G, nc, b, l, S5_GROUP).transpose(2, 1, 3, 0, 4).reshape(b, s, S5_W)


def _s5_post_kernel(y_ref, u_ref, d_ref, w_ref, b_ref, o_ref):
    y = y_ref[...] + d_ref[...] * u_ref[...]
    y = 0.5 * y * (1.0 + jnp.tanh(0.7978845608028654 * (y + 0.044715 * (y * y * y))))
    o_ref[...] = (y * _sigmoid(_dot(y, w_ref[...]) + b_ref[...])).astype(o_ref.dtype)


def _s5_post(y2, proj2, d, w_glu, b_glu):
    n = y2.shape[0]
    tm = min(2048, n)
    vec = pl.BlockSpec((1, S5_W), lambda i: (0, 0))
    return pl.pallas_call(
        _s5_post_kernel,
        out_shape=jax.ShapeDtypeStruct((n, S5_W), BF16),
        grid=(n // tm,),
        in_specs=[pl.BlockSpec((tm, S5_W), lambda i: (i, 0)),
                  pl.BlockSpec((tm, S5_W), lambda i: (i, ODD_IN // S5_W - 1)),
                  vec, pl.BlockSpec((S5_W, S5_W), lambda i: (0, 0)), vec],
        out_specs=pl.BlockSpec((tm, S5_W), lambda i: (i, 0)),
        compiler_params=_cparams(("parallel",)),
        name="s5_glu",
    )(y2, proj2, d.reshape(1, -1), w_glu.astype(BF16), b_glu.reshape(1, -1))


def _hgrn2_kernel(q_ref, f_ref, i_ref, gate_ref, loglb_ref, log1mlb_ref, onemlb_ref, gain_ref,
                  o_ref, st_scr):
    ts = q_ref.shape[1]
    lb = HG_BLOCK

    @pl.when(pl.program_id(1) == 0)
    def _():
        st_scr[...] = jnp.zeros_like(st_scr)

    tri_incl = (_iota2((lb, lb), 1) <= _iota2((lb, lb), 0)).astype(BF16)
    causal = _iota2((lb, lb), 1) <= _iota2((lb, lb), 0)

    def sub_chunk(c, carry):
        rows = pl.ds(pl.multiple_of(c * lb, lb), lb)
        for h in range(HG_HEADS):
            sl = slice(h * HG_DK, (h + 1) * HG_DK)
            q, fl, val = q_ref[0, rows, sl], f_ref[0, rows, sl], i_ref[0, rows, sl]
            log_sig = -_softplus(-fl)
            hi_arg = loglb_ref[:, sl]
            lo_arg = log1mlb_ref[:, sl] + log_sig
            log_f = jnp.maximum(hi_arg, lo_arg) + jnp.log1p(jnp.exp(-jnp.abs(hi_arg - lo_arg)))
            kk = onemlb_ref[:, sl] * _sigmoid(-fl)
            g_cum = _dot_sel_left(tri_incl, log_f)
            g_last = g_cum[lb - 1:lb, :]
            q_in = q * jnp.exp(g_cum)
            k_in = kk * jnp.exp(-g_cum)
            k_end = kk * jnp.exp(g_last - g_cum)
            att = jnp.where(causal, _dot_nt(q_in, k_in), 0.0)
            st = st_scr[h]
            o = _dot_nt(q_in, st) + _dot(att, val)
            st_scr[h] = st * jnp.exp(g_last) + _dot_tn(val, k_end)
            o = o * lax.rsqrt(jnp.mean(o * o, axis=-1, keepdims=True) + NORM_EPS) * gain_ref[:, sl]
            gate = gate_ref[0, rows, sl]
            o_ref[0, rows, sl] = (o * (gate * _sigmoid(gate))).astype(o_ref.dtype)
        return carry

    lax.fori_loop(0, ts // lb, sub_chunk, 0)


def _hgrn2(proj3, lb, gain):
    b, s, _ = proj3.shape
    ts = min(HG_TS, s)
    w = HG_KW
    lb = lb.astype(F32).reshape(1, w)
    col = lambda c: pl.BlockSpec((1, ts, w), lambda i, j: (i, j, c))
    vec = pl.BlockSpec((1, w), lambda i, j: (0, 0))
    return pl.pallas_call(
        _hgrn2_kernel,
        out_shape=jax.ShapeDtypeStruct((b, s, w), BF16),
        grid=(b, s // ts),
        in_specs=[col(0), col(1), col(2), col(3)] + [vec] * 4,
        out_specs=col(0),
        scratch_shapes=[pltpu.VMEM((HG_HEADS, HG_DK, HG_DK), F32)],
        compiler_params=_cparams(("parallel", "arbitrary")),
        name="hgrn2",
    )(proj3, proj3, proj3, proj3, jnp.log(lb), jnp.log1p(-lb), 1.0 - lb, gain.reshape(1, w))


def _pad_rows(w, rows, start):
    return jnp.zeros((rows, w.shape[1]), BF16).at[start:start + w.shape[0]].set(w.astype(BF16))


def _even_mixer(x2, bsz, seq, norm, w_in, w_out, mu, w0, w2, a0, a2, g2, k_k, k_a, r_k, ln_w, ln_b,
                q_gain, k_gain, v_first, v_mix):
    d = x2.shape[1]
    rw_cols = 3 * RW_W + W_LORA + A_LORA + G_LORA
    pad = RW_PAD - rw_cols
    extra = jnp.zeros((d, pad), w_in.dtype)
    mu_p = jnp.concatenate([mu, jnp.zeros((pad,), mu.dtype)])
    vres = None
    if v_mix is not None:
        v0, v1, v2 = v_mix
        extra = extra.at[:, :V_LORA].set(v1)
        vres = (v0, _pad_rows(v2, RW_SMALL, W_LORA + A_LORA + G_LORA), v_first)
    w_pad = jnp.concatenate([w_in[:, :rw_cols], extra, w_in[:, rw_cols:]], axis=1)
    proj3 = _proj(x2, norm, w_pad).reshape(bsz, seq, EVEN_PAD)
    r, k, v, lw, kk, a, g = _rwkv_prep(
        proj3, mu_p, w0, _pad_rows(w2, RW_SMALL, 0), a0, _pad_rows(a2, RW_SMALL, W_LORA),
        _pad_rows(g2, RW_SMALL, W_LORA + A_LORA), k_k, k_a, vres)
    y_rw = _rwkv_chunk(r, k, v, lw, kk, a, g, r_k, ln_w, ln_b)
    y_sb = _sb_attention(proj3, q_gain, k_gain, RW_PAD // LANES)
    n = bsz * seq
    out = _outproj(x2, y_rw.reshape(n, RW_W), y_sb.reshape(n, SB_W), w_out)
    return out, v


def _odd_mixer(x2, bsz, seq, norm, w_in, w_out, s5_params, d_skip, w_glu, b_glu, lb, hg_gain):
    n = bsz * seq
    w_perm = jnp.concatenate([w_in[:, S5_W:], w_in[:, :S5_W]], axis=1)
    proj2 = _proj(x2, norm, w_perm)
    proj3 = proj2.reshape(bsz, seq, ODD_IN)
    tables = _s5_tables(*s5_params, min(S5_CHUNK, seq))
    ssm = _s5_ssm(proj3[..., ODD_IN - S5_W:], tables)
    y_s5 = _s5_post(ssm.reshape(n, S5_W), proj2, d_skip, w_glu, b_glu)
    y_hg = _hgrn2(proj3, lb, hg_gain)
    return _outproj(x2, y_s5, y_hg.reshape(n, HG_VW), w_out)


def kernel(x, ffn1_norm, ffn1_w13, ffn1_w2, mix_norm, ffn2_norm, ffn2_w13, ffn2_w2, ev_w_in, ev_w_out, rw_mu, rw_w0, rw_w2, rw_a0, rw_a2, rw_g2, rw_k_k, rw_k_a, rw_r_k, rw_ln_w, rw_ln_b, rw_v0, rw_v1, rw_v2, sb_q_gain, sb_k_gain, od_w_in, od_w_out, s5_a_re, s5_a_im, s5_b_re, s5_b_im, s5_c_re, s5_c_im, s5_d, s5_log_dt, s5_w_glu, s5_b_glu, hg_lb, hg_gain):
    bsz, seq, d = x.shape
    depth = ffn1_norm.shape[0]
    lb_all = jnp.cumsum(jax.nn.softmax(hg_lb.astype(F32), axis=0), axis=0)
    lb_all = lb_all - lb_all[0]
    x2 = x.reshape(bsz * seq, d)
    v_first = None
    for layer in range(depth):
        x2 = _ffn(x2, ffn1_norm[layer], ffn1_w13[layer], ffn1_w2[layer])
        if layer % 2 == 0:
            e = layer // 2
            v_mix = None if e == 0 else (rw_v0[e - 1], rw_v1[e - 1], rw_v2[e - 1])
            x2, v = _even_mixer(x2, bsz, seq, mix_norm[layer], ev_w_in[e], ev_w_out[e], rw_mu[e], rw_w0[e],
                                rw_w2[e], rw_a0[e], rw_a2[e], rw_g2[e], rw_k_k[e], rw_k_a[e], rw_r_k[e],
                                rw_ln_w[e], rw_ln_b[e], sb_q_gain[e], sb_k_gain[e], v_first, v_mix)
            if e == 0:
                v_first = v
        else:
            o = layer // 2
            s5_params = (s5_a_re[o], s5_a_im[o], s5_b_re[o], s5_b_im[o], s5_c_re[o], s5_c_im[o], s5_log_dt[o])
            x2 = _odd_mixer(x2, bsz, seq, mix_norm[layer], od_w_in[o], od_w_out[o], s5_params, s5_d[o],
                            s5_w_glu[o], s5_b_glu[o], lb_all[o], hg_gain[o])
        x2 = _ffn(x2, ffn2_norm[layer], ffn2_w13[layer], ffn2_w2[layer])
    return x2.reshape(bsz, seq, d)
```

```python
import functools

import jax
import jax.numpy as jnp
from jax import lax
from jax.experimental import pallas as pl
from jax.experimental.pallas import tpu as pltpu

F32 = jnp.float32
BF16 = jnp.bfloat16

LOG2E = 1.4426950408889634
NORM_EPS = 1e-6
GN_EPS = 64e-5
LANES = 128
RW_HEADS, RW_HD, RW_W = 8, 64, 512
W_LORA, A_LORA, V_LORA, G_LORA = 32, 32, 32, 96
RW_SMALL = 256
RW_PAD = 3 * RW_W + RW_SMALL
SB_W, SB_HD = 512, 64
EVEN_PAD = RW_PAD + 3 * SB_W
S5_G, S5_GROUP, S5_W, S5_P = 16, 16, 256, 64
HG_HEADS, HG_DK, HG_KW, HG_VW, HG_BLOCK = 6, 128, 768, 768, 16
ODD_IN = S5_W + 2 * HG_KW + 2 * HG_VW
VMEM_LIMIT = 48 * 1024 * 1024

RW_CHUNK = 64
S5_CHUNK = 64
SB_TQ = 512
SB_TK = 128
HG_TS = 128


def _cparams(sem):
    return pltpu.CompilerParams(dimension_semantics=sem, vmem_limit_bytes=VMEM_LIMIT)


def _dot(a, b):
    return jnp.dot(a.astype(BF16), b.astype(BF16), preferred_element_type=F32)


def _dot_nt(a, b):
    return lax.dot_general(a.astype(BF16), b.astype(BF16), (((1,), (1,)), ((), ())),
                           preferred_element_type=F32)


def _dot_tn(a, b):
    return lax.dot_general(a.astype(BF16), b.astype(BF16), (((0,), (0,)), ((), ())),
                           preferred_element_type=F32)


def _dot_hi(a, b):
    return jnp.dot(a, b, preferred_element_type=F32, precision=lax.Precision.HIGHEST)


def _split(x):
    hi = x.astype(BF16)
    lo = (x - hi.astype(F32)).astype(BF16)
    return hi, lo


def _dot_sel_right(x, e):
    hi, lo = _split(x)
    return (jnp.dot(hi, e, preferred_element_type=F32) + jnp.dot(lo, e, preferred_element_type=F32))


def _dot_sel_left(e, x):
    hi, lo = _split(x)
    return (jnp.dot(e, hi, preferred_element_type=F32) + jnp.dot(e, lo, preferred_element_type=F32))


def _softplus(x):
    return jnp.maximum(x, 0.0) + jnp.log(1.0 + jnp.exp(-jnp.abs(x)))


def _sigmoid(x):
    return jax.nn.sigmoid(x)


def _iota2(shape, dim):
    return lax.broadcasted_iota(jnp.int32, shape, dim)


def _group_ones(n, group):
    return (_iota2((n, n), 0) // group == _iota2((n, n), 1) // group).astype(BF16)


def _rms_rows(x, gain):
    return x * lax.rsqrt(jnp.mean(x * x, axis=-1, keepdims=True) + NORM_EPS) * gain


def _ffn_kernel(x_ref, g_ref, w1_ref, w3_ref, w2_ref, o_ref, h_scr, acc_scr):
    j = pl.program_id(1)

    @pl.when(j == 0)
    def _():
        h_scr[...] = _rms_rows(x_ref[...], g_ref[...]).astype(BF16)
        acc_scr[...] = jnp.zeros_like(acc_scr)

    h = h_scr[...]
    gate = jnp.dot(h, w1_ref[...], preferred_element_type=F32)
    up = jnp.dot(h, w3_ref[...], preferred_element_type=F32)
    act = (gate * _sigmoid(gate) * up).astype(BF16)
    acc_scr[...] += jnp.dot(act, w2_ref[...], preferred_element_type=F32)

    @pl.when(j == pl.num_programs(1) - 1)
    def _():
        o_ref[...] = x_ref[...] + 0.5 * acc_scr[...]


def _ffn(x2, gain, w13, w2):
    n, d = x2.shape
    dff = w2.shape[0]
    tm = min(1024, n)
    tf = 256
    nf = dff // tf
    return pl.pallas_call(
        _ffn_kernel,
        out_shape=jax.ShapeDtypeStruct((n, d), F32),
        grid=(n // tm, nf),
        in_specs=[
            pl.BlockSpec((tm, d), lambda i, j: (i, 0)),
            pl.BlockSpec((1, d), lambda i, j: (0, 0)),
            pl.BlockSpec((d, tf), lambda i, j: (0, j)),
            pl.BlockSpec((d, tf), lambda i, j: (0, j + nf)),
            pl.BlockSpec((tf, d), lambda i, j: (j, 0)),
        ],
        out_specs=pl.BlockSpec((tm, d), lambda i, j: (i, 0)),
        scratch_shapes=[pltpu.VMEM((tm, d), BF16), pltpu.VMEM((tm, d), F32)],
        compiler_params=_cparams(("parallel", "arbitrary")),
        name="ffn",
    )(x2, gain.reshape(1, d), w13.astype(BF16), w13.astype(BF16), w2.astype(BF16))


def _proj_kernel(x_ref, g_ref, w_ref, o_ref, h_scr):
    @pl.when(pl.program_id(1) == 0)
    def _():
        h_scr[...] = _rms_rows(x_ref[...], g_ref[...]).astype(BF16)

    o_ref[...] = jnp.dot(h_scr[...], w_ref[...], preferred_element_type=F32)


def _proj(x2, gain, w):
    n, d = x2.shape
    c = w.shape[1]
    tm = min(1024, n)
    tn = c // 2
    return pl.pallas_call(
        _proj_kernel,
        out_shape=jax.ShapeDtypeStruct((n, c), F32),
        grid=(n // tm, c // tn),
        in_specs=[
            pl.BlockSpec((tm, d), lambda i, j: (i, 0)),
            pl.BlockSpec((1, d), lambda i, j: (0, 0)),
            pl.BlockSpec((d, tn), lambda i, j: (0, j)),
        ],
        out_specs=pl.BlockSpec((tm, tn), lambda i, j: (i, j)),
        scratch_shapes=[pltpu.VMEM((tm, d), BF16)],
        compiler_params=_cparams(("parallel", "arbitrary")),
        name="mixer_in_proj",
    )(x2, gain.reshape(1, d), w.astype(BF16))


def _outproj_kernel(x_ref, a1_ref, a2_ref, w1_ref, w2_ref, o_ref):
    o_ref[...] = (x_ref[...]
                  + jnp.dot(a1_ref[...], w1_ref[...], preferred_element_type=F32)
                  + jnp.dot(a2_ref[...], w2_ref[...], preferred_element_type=F32))


def _outproj(x2, a1, a2, w_out):
    n, d = x2.shape
    k1, k2 = a1.shape[1], a2.shape[1]
    tm = min(1024, n)
    w = w_out.astype(BF16)
    return pl.pallas_call(
        _outproj_kernel,
        out_shape=jax.ShapeDtypeStruct((n, d), F32),
        grid=(n // tm,),
        in_specs=[
            pl.BlockSpec((tm, d), lambda i: (i, 0)),
            pl.BlockSpec((tm, k1), lambda i: (i, 0)),
            pl.BlockSpec((tm, k2), lambda i: (i, 0)),
            pl.BlockSpec((k1, d), lambda i: (0, 0)),
            pl.BlockSpec((k2, d), lambda i: (0, 0)),
        ],
        out_specs=pl.BlockSpec((tm, d), lambda i: (i, 0)),
        compiler_params=_cparams(("parallel",)),
        name="mixer_out_proj",
    )(x2, a1, a2, w[:k1], w[k1:])


def _rwkv_prep_kernel(*refs, has_vres):
    if has_vres:
        (p_ref, mu_ref, w0_ref, w2_ref, a0_ref, a2_ref, g2_ref, kk_ref, ka_ref, v0_ref, v2_ref, vf_ref,
         r_o, k_o, v_o, lw_o, kk_o, a_o, g_o, carry) = refs
    else:
        (p_ref, mu_ref, w0_ref, w2_ref, a0_ref, a2_ref, g2_ref, kk_ref, ka_ref,
         r_o, k_o, v_o, lw_o, kk_o, a_o, g_o, carry) = refs
    ts = p_ref.shape[1]

    @pl.when(pl.program_id(1) == 0)
    def _():
        carry[...] = jnp.zeros_like(carry)

    x = p_ref[0]
    prev = pltpu.roll(x, 1, axis=0)
    prev = jnp.where(_iota2((ts, 1), 0) == 0, carry[...], prev)
    carry[...] = x[ts - 1:ts, :]
    rw = x + mu_ref[...] * (prev - x)

    r = rw[:, 0:RW_W]
    k = rw[:, RW_W:2 * RW_W]
    v = rw[:, 2 * RW_W:3 * RW_W]
    small = rw[:, 3 * RW_W:RW_PAD]

    log_w = -_softplus(-(w0_ref[...] + _dot(jnp.tanh(small), w2_ref[...]))) - 0.5
    a = _sigmoid(a0_ref[...] + _dot(small, a2_ref[...]))
    g = _dot(_sigmoid(small), g2_ref[...])
    if has_vres:
        v = v + (vf_ref[0] - v) * _sigmoid(v0_ref[...] + _dot(small, v2_ref[...]))

    kk = k * kk_ref[...]
    ss = _dot_sel_right(kk * kk, _group_ones(RW_W, RW_HD))
    kk = kk * lax.rsqrt(jnp.maximum(ss, 1e-24))

    r_o[0] = r
    k_o[0] = k * (1.0 + (a - 1.0) * ka_ref[...])
    v_o[0] = v
    lw_o[0] = -jnp.exp(log_w)
    kk_o[0] = kk
    a_o[0] = a
    g_o[0] = g


def _rwkv_prep(proj3, mu, w0, w2p, a0, a2p, g2p, k_k, k_a, vres):
    b, s, _ = proj3.shape
    ts = min(512, s)
    has_vres = vres is not None
    row = lambda t: t.reshape(1, -1)
    vec = lambda n: pl.BlockSpec((1, n), lambda i, j: (0, 0))
    lora = pl.BlockSpec((RW_SMALL, RW_W), lambda i, j: (0, 0))
    seq = pl.BlockSpec((1, ts, RW_W), lambda i, j: (i, j, 0))
    args = [proj3, row(mu), row(w0), w2p, row(a0), a2p, g2p, row(k_k), row(k_a)]
    specs = [pl.BlockSpec((1, ts, RW_PAD), lambda i, j: (i, j, 0)), vec(RW_PAD), vec(RW_W), lora,
             vec(RW_W), lora, lora, vec(RW_W), vec(RW_W)]
    if has_vres:
        v0, v2p, v_first = vres
        args += [row(v0), v2p, v_first]
        specs += [vec(RW_W), lora, seq]
    out = jax.ShapeDtypeStruct((b, s, RW_W), F32)
    return pl.pallas_call(
        functools.partial(_rwkv_prep_kernel, has_vres=has_vres),
        out_shape=[out] * 7,
        grid=(b, s // ts),
        in_specs=specs,
        out_specs=[seq] * 7,
        scratch_shapes=[pltpu.VMEM((1, RW_PAD), F32)],
        compiler_params=_cparams(("parallel", "arbitrary")),
        name="rwkv_prep",
    )(*args)


def _dot3(a, b):
    ah, al = a if isinstance(a, tuple) else _split(a)
    bh, bl = b if isinstance(b, tuple) else _split(b)
    d = lambda x, y: jnp.dot(x, y, preferred_element_type=F32)
    return d(ah, bh) + (d(ah, bl) + d(al, bh))


def _unit_lower_inverses(ns, l):
    size = ns[0].shape[0]
    row, col = _iota2((size, size), 0), _iota2((size, size), 1)
    eye = (row == col).astype(F32)
    same = lambda blk: (row // blk) == (col // blk)
    n1 = [_split(jnp.where(same(8), n, 0.0)) for n in ns]
    n2 = [_dot3(a, a) for a in n1]
    n2s = [_split(a) for a in n2]
    n4 = [_dot3(a, a) for a in n2s]
    n3 = [_dot3(a, b) for a, b in zip(n1, n2s)]
    p1 = [eye + jnp.where(same(8), n, 0.0) + b + c for n, b, c in zip(ns, n2, n3)]
    t = [p + _dot3(p, q) for p, q in zip(p1, n4)]
    blk = 8
    while blk < l:
        level = same(2 * blk) & jnp.logical_not(same(blk))
        ts = [_split(a) for a in t]
        tn = [_dot3(a, jnp.where(level, n, 0.0)) for a, n in zip(ts, ns)]
        t = [a + _dot3(b, c) for a, b, c in zip(t, tn, ts)]
        blk *= 2
    return t


def _rwkv_chunk_kernel(r_ref, k_ref, v_ref, lw_ref, kk_ref, a_ref, g_ref, rk_ref, lnw_ref, lnb_ref,
                       o_ref, h_scr):
    l = r_ref.shape[1]

    @pl.when(pl.program_id(1) == 0)
    def _():
        h_scr[...] = jnp.zeros_like(h_scr)

    pairs = range(RW_W // LANES)
    sls = [slice(p * LANES, (p + 1) * LANES) for p in pairs]
    tri_incl = (_iota2((l, l), 1) <= _iota2((l, l), 0)).astype(BF16)
    m0 = _iota2((1, LANES), 1) < RW_HD
    rr, cc = _iota2((2 * l, 2 * l), 0), _iota2((2 * l, 2 * l), 1)
    same_head = (rr // l) == (cc // l)
    strict = same_head & ((rr % l) > (cc % l))
    incl = same_head & ((rr % l) >= (cc % l))
    eye = _iota2((LANES, LANES), 0) == _iota2((LANES, LANES), 1)
    head_ones = _group_ones(LANES, RW_HD)
    stack = lambda t: jnp.concatenate([jnp.where(m0, t, 0.0), jnp.where(m0, 0.0, t)], axis=0)
    unstack = lambda t: t[:l] + t[l:]

    r = [r_ref[0, :, s] for s in sls]
    k = [k_ref[0, :, s] for s in sls]
    v = [v_ref[0, :, s] for s in sls]
    gc = [_dot_sel_left(tri_incl, lw_ref[0, :, s]) for s in sls]
    g_last = [g[l - 1:l, :] for g in gc]
    e_neg = [jnp.exp(-g) for g in gc]
    e_end = [jnp.exp(gl - g) for gl, g in zip(g_last, gc)]
    bv = [kk_ref[0, :, s] * a_ref[0, :, s] for s in sls]
    a_s = [stack(-kk_ref[0, :, s] * jnp.exp(g - lw_ref[0, :, s])) for s, g in zip(sls, gc)]
    r_s = [stack(x * jnp.exp(g)) for x, g in zip(r, gc)]
    b_s = [stack(x * e) for x, e in zip(bv, e_neg)]
    k_s = [stack(x * e) for x, e in zip(k, e_neg)]
    v_s = [stack(x) for x in v]
    end_s = [jnp.concatenate([stack(x * e), stack(y * e)], axis=0) for x, y, e in zip(bv, k, e_end)]

    prod = [_dot_nt(jnp.concatenate([a, rq], axis=0), jnp.concatenate([b, kq], axis=0))
            for a, rq, b, kq in zip(a_s, r_s, b_s, k_s)]
    t_inv = _unit_lower_inverses([jnp.where(strict, x[:2 * l, :2 * l], 0.0) for x in prod], l)
    x_s = [_dot(jnp.where(strict, pr[:2 * l, 2 * l:], 0.0), vs) for pr, vs in zip(prod, v_s)]
    wu = [_dot3(t, jnp.concatenate([a, x], axis=1)) for t, a, x in zip(t_inv, a_s, x_s)]
    big = [jnp.concatenate([w, jnp.concatenate([jnp.zeros_like(vs), vs], axis=1)], axis=0)
           for w, vs in zip(wu, v_s)]
    r_bk = [jnp.concatenate([jnp.where(incl, pr[2 * l:, :2 * l], 0.0), jnp.where(incl, pr[2 * l:, 2 * l:], 0.0)],
                            axis=1) for pr in prod]
    qy = [_dot(x, b) for x, b in zip(r_bk, big)]
    mn = [_dot_tn(e, b) for e, b in zip(end_s, big)]

    h_prev = [h_scr[p] for p in pairs]
    q_s = [rq + x[:, :LANES] for rq, x in zip(r_s, qy)]
    y = [unstack(_dot(q, h) + x[:, LANES:]) for q, h, x in zip(q_s, h_prev, qy)]
    for p in pairs:
        m_mat = mn[p][:, :LANES] + jnp.where(eye, jnp.broadcast_to(jnp.exp(g_last[p]), (LANES, LANES)), 0.0)
        h_scr[p] = _dot3(m_mat, h_prev[p]) + mn[p][:, LANES:]

    for p in pairs:
        sl = sls[p]
        mean = _dot_sel_right(y[p], head_ones) * (1.0 / RW_HD)
        yc = y[p] - mean
        var = _dot_sel_right(yc * yc, head_ones) * (1.0 / RW_HD)
        yn = yc * lax.rsqrt(var + GN_EPS) * lnw_ref[:, sl] + lnb_ref[:, sl]
        bonus = _dot_sel_right(r[p] * k[p] * rk_ref[:, sl], head_ones) * v[p]
        o_ref[0, :, sl] = ((yn + bonus) * g_ref[0, :, sl]).astype(o_ref.dtype)


def _rwkv_chunk(r, k, v, lw, kk, a, g, r_k, ln_w, ln_b):
    b, s, _ = r.shape
    l = RW_CHUNK
    seq = pl.BlockSpec((1, l, RW_W), lambda i, j: (i, j, 0))
    vec = pl.BlockSpec((1, RW_W), lambda i, j: (0, 0))
    return pl.pallas_call(
        _rwkv_chunk_kernel,
        out_shape=jax.ShapeDtypeStruct((b, s, RW_W), BF16),
        grid=(b, s // l),
        in_specs=[seq] * 7 + [vec] * 3,
        out_specs=seq,
        scratch_shapes=[pltpu.VMEM((RW_W // LANES, LANES, LANES), F32)],
        compiler_params=_cparams(("parallel", "arbitrary")),
        name="rwkv_chunk",
    )(r, k, v, lw, kk, a, g, r_k.reshape(1, -1), ln_w.reshape(1, -1), ln_b.reshape(1, -1))


def _pair_rmsnorm(t, gain_row, ones):
    ms = _dot_sel_right(t * t, ones) * (1.0 / SB_HD)
    return t * lax.rsqrt(ms + NORM_EPS) * gain_row


def _sb_kernel(q_ref, k_ref, v_ref, qg_ref, kg_ref, o_ref, kn_scr, vb_scr, qs_scr, c_scr, acc_scr):
    tq = q_ref.shape[1]
    tk = SB_TK
    qb = pl.program_id(2)
    ones = _group_ones(LANES, SB_HD)

    @pl.when(qb == 0)
    def _():
        kn_scr[...] = _pair_rmsnorm(k_ref[0], kg_ref[...], ones).astype(BF16)
        vb_scr[...] = v_ref[0].astype(BF16)

    q = _pair_rmsnorm(q_ref[0], qg_ref[...], ones) * (SB_HD ** -0.5)
    m0 = _iota2((1, LANES), 1) < SB_HD
    qs_scr[0:tq, :] = jnp.where(m0, q, 0.0).astype(BF16)
    qs_scr[tq:2 * tq, :] = jnp.where(m0, 0.0, q).astype(BF16)
    c_scr[...] = jnp.zeros_like(c_scr)
    acc_scr[...] = jnp.zeros_like(acc_scr)

    tri_after = (_iota2((tk, tk), 0) > _iota2((tk, tk), 1)).astype(BF16)
    sel = jnp.concatenate([tri_after, jnp.ones((tk, LANES), BF16)], axis=1)
    sel = jnp.concatenate([sel, sel], axis=0)

    def tile_pair(j_old, masked):
        start = pl.multiple_of(j_old * tk, tk)
        kk = kn_scr[pl.ds(start, 2 * tk), :]
        vv = vb_scr[pl.ds(start, 2 * tk), :]
        z = lax.dot_general(qs_scr[...], kk, (((1,), (1,)), ((), ())), preferred_element_type=F32)
        drop = jnp.maximum(z, 0.0) + jnp.log(1.0 + jnp.exp2(jnp.abs(z) * (-LOG2E)))
        if masked:
            r = _iota2((2 * tq, 2 * tk), 0)
            q_pos = qb * tq + jnp.where(r >= tq, r - tq, r)
            before = (start + _iota2((2 * tq, 2 * tk), 1)) < q_pos
            drop = jnp.where(before, drop, 0.0)
        hi, lo = _split(drop)
        x_new = jnp.dot(jnp.concatenate([hi[:, tk:], lo[:, tk:]], axis=1), sel, preferred_element_type=F32)
        x_old = jnp.dot(jnp.concatenate([hi[:, :tk], lo[:, :tk]], axis=1), sel, preferred_element_type=F32)
        c = c_scr[...]
        c_old = c + x_new[:, tk:]
        w_new = jnp.exp(z[:, tk:] - (drop[:, tk:] + x_new[:, :tk] + c))
        w_old = jnp.exp(z[:, :tk] - (drop[:, :tk] + x_old[:, :tk] + c_old))
        wgt = jnp.concatenate([w_old, w_new], axis=1)
        if masked:
            wgt = jnp.where(before, wgt, 0.0)
        c_scr[...] = c_old + x_old[:, tk:]
        acc_scr[...] += jnp.dot(wgt.astype(BF16), vv, preferred_element_type=F32)

    tiles_per_block = tq // tk
    for i in reversed(range(tiles_per_block // 2)):
        tile_pair(qb * tiles_per_block + 2 * i, True)

    def body(it, carry):
        tile_pair(qb * tiles_per_block - 2 - 2 * it, False)
        return carry

    lax.fori_loop(0, qb * (tiles_per_block // 2), body, 0)
    o_ref[0] = jnp.where(m0, acc_scr[0:tq, :], acc_scr[tq:2 * tq, :]).astype(o_ref.dtype)


def _sb_attention(proj3, q_gain, k_gain, col0):
    b, s, _ = proj3.shape
    tq = min(SB_TQ, s)
    assert tq % (2 * SB_TK) == 0 and s % tq == 0
    npair = SB_W // LANES
    gain2 = lambda g: jnp.tile(g, 2).reshape(1, LANES)
    return pl.pallas_call(
        _sb_kernel,
        out_shape=jax.ShapeDtypeStruct((b, s, SB_W), BF16),
        grid=(b, npair, s // tq),
        in_specs=[
            pl.BlockSpec((1, tq, LANES), lambda i, p, j: (i, j, col0 + p)),
            pl.BlockSpec((1, s, LANES), lambda i, p, j: (i, 0, col0 + npair + p)),
            pl.BlockSpec((1, s, LANES), lambda i, p, j: (i, 0, col0 + 2 * npair + p)),
            pl.BlockSpec((1, LANES), lambda i, p, j: (0, 0)),
            pl.BlockSpec((1, LANES), lambda i, p, j: (0, 0)),
        ],
        out_specs=pl.BlockSpec((1, tq, LANES), lambda i, p, j: (i, j, p)),
        scratch_shapes=[pltpu.VMEM((s, LANES), BF16), pltpu.VMEM((s, LANES), BF16),
                        pltpu.VMEM((2 * tq, LANES), BF16), pltpu.VMEM((2 * tq, LANES), F32),
                        pltpu.VMEM((2 * tq, LANES), F32)],
        compiler_params=_cparams(("parallel", "parallel", "arbitrary")),
        name="stick_breaking",
    )(proj3, proj3, proj3, gain2(q_gain), gain2(k_gain))


def _s5_tables(a_re, a_im, b_re, b_im, c_re, c_im, log_dt, l):
    lam = lax.complex(jnp.minimum(a_re.astype(F32), -1e-4), a_im.astype(F32))
    lam_dt = lam * jnp.exp(log_dt.astype(F32))[:, None]
    lam_bar = jnp.exp(lam_dt)
    b_bar = ((lam_bar - 1.0) / lam)[..., None] * lax.complex(b_re.astype(F32), b_im.astype(F32))
    c = lax.complex(c_re.astype(F32), c_im.astype(F32))
    steps = jnp.arange(l + 1, dtype=F32)
    powers = jnp.exp(lam_dt[:, None, :] * steps[None, :, None])
    g, p = lam.shape
    kern = jnp.einsum('gcp,gjp,gpd->gjcd', c, powers[:, :l], b_bar).real
    lag = jnp.arange(l)[None, :] - jnp.arange(l)[:, None]
    toep = jnp.where((lag >= 0)[None, :, :, None, None], kern[:, jnp.clip(lag, 0, l - 1)], 0.0)
    toep = toep.transpose(0, 1, 4, 2, 3).reshape(g, l * S5_GROUP, l * S5_GROUP)
    b_end = powers[:, :l][:, ::-1, :, None] * b_bar[:, None, :, :]
    b_end = b_end.transpose(0, 1, 3, 2).reshape(g, l * S5_GROUP, p)
    b_end = jnp.concatenate([b_end.real, b_end.imag], axis=-1)
    c_pow = c[:, None, :, :] * powers[:, 1:, None, :]
    c_pow = jnp.concatenate([c_pow.real, -c_pow.imag], axis=-1)
    c_pow = c_pow.transpose(0, 3, 1, 2).reshape(g, 2 * p, l * S5_GROUP)
    lam_l = powers[:, l]
    mul_same = jnp.concatenate([lam_l.real, lam_l.real], axis=-1).reshape(g, 1, 2 * p)
    mul_swap = jnp.concatenate([-lam_l.imag, lam_l.imag], axis=-1).reshape(g, 1, 2 * p)
    return toep.astype(BF16), b_end.astype(BF16), c_pow.astype(BF16), mul_same, mul_swap


def _s5_kernel(u_ref, toep_ref, bend_ref, cpow_ref, same_ref, swap_ref, y_ref, e_scr, x_scr, *, bsz):
    u = u_ref[0].astype(BF16)
    e_scr[...] = jnp.dot(u, bend_ref[0], preferred_element_type=F32)
    n_chunks = u.shape[0] // bsz
    same, swap = same_ref[0], swap_ref[0]

    def body(c, x):
        rows = pl.ds(pl.multiple_of(c * bsz, bsz), bsz)
        x_scr[rows, :] = x
        return x * same + pltpu.roll(x, S5_P, axis=1) * swap + e_scr[rows, :]

    lax.fori_loop(0, n_chunks, body, jnp.zeros((bsz, 2 * S5_P), F32))
    y_ref[0] = (jnp.dot(u, toep_ref[0], preferred_element_type=F32)
                + jnp.dot(x_scr[...].astype(BF16), cpow_ref[0], preferred_element_type=F32))


def _s5_ssm(u, tables):
    b, s, _ = u.shape
    l = min(S5_CHUNK, s)
    nc = s // l
    toep, b_end, c_pow, mul_same, mul_swap = tables
    width = l * S5_GROUP
    ug = u.reshape(b, nc, l, S5_G, S5_GROUP).transpose(3, 1, 0, 2, 4).reshape(S5_G, nc * b, width)
    grp = lambda shape: pl.BlockSpec((1,) + shape, lambda g: (g, 0, 0))
    y = pl.pallas_call(
        functools.partial(_s5_kernel, bsz=b),
        out_shape=jax.ShapeDtypeStruct((S5_G, nc * b, width), F32),
        grid=(S5_G,),
        in_specs=[grp((nc * b, width)), grp((width, width)), grp((width, 2 * S5_P)),
                  grp((2 * S5_P, width)), grp((1, 2 * S5_P)), grp((1, 2 * S5_P))],
        out_specs=grp((nc * b, width)),
        scratch_shapes=[pltpu.VMEM((nc * b, 2 * S5_P), F32), pltpu.VMEM((nc * b, 2 * S5_P), F32)],
        compiler_params=_cparams(("parallel",)),
        name="s5_ssm",
    )(ug, toep, b_end, c_pow, mul_same, mul_swap)
    return y.reshape(S5_G, nc, b, l, S5_GROUP).transpose(2, 1, 3, 0, 4).reshape(b, s, S5_W)


def _s5_post_kernel(y_ref, u_ref, d_ref, w_ref, b_ref, o_ref):
    y = y_ref[...] + d_ref[...] * u_ref[...]
    y = 0.5 * y * (1.0 + jnp.tanh(0.7978845608028654 * (y + 0.044715 * (y * y * y))))
    o_ref[...] = (y * _sigmoid(_dot(y, w_ref[...]) + b_ref[...])).astype(o_ref.dtype)


def _s5_post(y2, proj2, d, w_glu, b_glu):
    n = y2.shape[0]
    tm = min(2048, n)
    vec = pl.BlockSpec((1, S5_W), lambda i: (0, 0))
    return pl.pallas_call(
        _s5_post_kernel,
        out_shape=jax.ShapeDtypeStruct((n, S5_W), BF16),
        grid=(n // tm,),
        in_specs=[pl.BlockSpec((tm, S5_W), lambda i: (i, 0)),
                  pl.BlockSpec((tm, S5_W), lambda i: (i, ODD_IN // S5_W - 1)),
                  vec, pl.BlockSpec((S5_W, S5_W), lambda i: (0, 0)), vec],
        out_specs=pl.BlockSpec((tm, S5_W), lambda i: (i, 0)),
        compiler_params=_cparams(("parallel",)),
        name="s5_glu",
    )(y2, proj2, d.reshape(1, -1), w_glu.astype(BF16), b_glu.reshape(1, -1))


def _hgrn2_kernel(q_ref, f_ref, i_ref, gate_ref, loglb_ref, log1mlb_ref, onemlb_ref, gain_ref,
                  o_ref, st_scr):
    ts = q_ref.shape[1]
    lb = HG_BLOCK
    heads = range(HG_HEADS)
    sls = [slice(h * HG_DK, (h + 1) * HG_DK) for h in heads]

    @pl.when(pl.program_id(1) == 0)
    def _():
        st_scr[...] = jnp.zeros_like(st_scr)

    rr, cc = _iota2((ts, ts), 0), _iota2((ts, ts), 1)
    same = (rr // lb) == (cc // lb)
    causal = same & (cc <= rr)
    tri, blk = causal.astype(BF16), same.astype(BF16)
    sel = jnp.concatenate([jnp.concatenate([tri, tri], axis=1), jnp.concatenate([blk, blk], axis=1)], axis=0)

    fl = f_ref[0]
    hi_arg = loglb_ref[...]
    lo_arg = log1mlb_ref[...] - _softplus(-fl)
    log_f = jnp.maximum(hi_arg, lo_arg) + jnp.log(1.0 + jnp.exp(-jnp.abs(hi_arg - lo_arg)))
    kk = onemlb_ref[...] * _sigmoid(-fl)
    hi, lo = _split(log_f)
    gg = jnp.dot(sel, jnp.concatenate([hi, lo], axis=0), preferred_element_type=F32)
    g_cum, g_tot = gg[:ts], gg[ts:]
    q_in = q_ref[0] * jnp.exp(g_cum)
    k_in = kk * jnp.exp(-g_cum)
    k_end = kk * jnp.exp(g_tot - g_cum)
    decay = jnp.exp(g_tot)
    val = i_ref[0]
    att = [jnp.where(causal, _dot_nt(q_in[:, s], k_in[:, s]), 0.0) for s in sls]
    intra = [_dot(a, val[:, s]) for a, s in zip(att, sls)]
    n_sub = ts // lb
    rows = [slice(c * lb, (c + 1) * lb) for c in range(n_sub)]
    push = [[_dot_tn(val[r, s], k_end[r, s]) for s in sls] for r in rows]

    st = [st_scr[h] for h in heads]
    inter = []
    for c, r in enumerate(rows):
        inter.append([_dot_nt(q_in[r, s], x) for s, x in zip(sls, st)])
        st = [x * decay[c * lb:c * lb + 1, s] + p for x, s, p in zip(st, sls, push[c])]
    for h in heads:
        st_scr[h] = st[h]

    gate = gate_ref[0]
    for h, s in zip(heads, sls):
        o = intra[h] + jnp.concatenate([inter[c][h] for c in range(n_sub)], axis=0)
        o = o * lax.rsqrt(jnp.mean(o * o, axis=-1, keepdims=True) + NORM_EPS) * gain_ref[:, s]
        o_ref[0, :, s] = (o * (gate[:, s] * _sigmoid(gate[:, s]))).astype(o_ref.dtype)


def _hgrn2(proj3, lb, gain):
    b, s, _ = proj3.shape
    ts = min(HG_TS, s)
    w = HG_KW
    lb = lb.astype(F32).reshape(1, w)
    col = lambda c: pl.BlockSpec((1, ts, w), lambda i, j: (i, j, c))
    vec = pl.BlockSpec((1, w), lambda i, j: (0, 0))
    return pl.pallas_call(
        _hgrn2_kernel,
        out_shape=jax.ShapeDtypeStruct((b, s, w), BF16),
        grid=(b, s // ts),
        in_specs=[col(0), col(1), col(2), col(3)] + [vec] * 4,
        out_specs=col(0),
        scratch_shapes=[pltpu.VMEM((HG_HEADS, HG_DK, HG_DK), F32)],
        compiler_params=_cparams(("parallel", "arbitrary")),
        name="hgrn2",
    )(proj3, proj3, proj3, proj3, jnp.log(lb), jnp.log1p(-lb), 1.0 - lb, gain.reshape(1, w))


def _pad_rows(w, rows, start):
    return jnp.zeros((rows, w.shape[1]), BF16).at[start:start + w.shape[0]].set(w.astype(BF16))


def _even_mixer(x2, bsz, seq, norm, w_in, w_out, mu, w0, w2, a0, a2, g2, k_k, k_a, r_k, ln_w, ln_b,
                q_gain, k_gain, v_first, v_mix):
    d = x2.shape[1]
    rw_cols = 3 * RW_W + W_LORA + A_LORA + G_LORA
    pad = RW_PAD - rw_cols
    extra = jnp.zeros((d, pad), w_in.dtype)
    mu_p = jnp.concatenate([mu, jnp.zeros((pad,), mu.dtype)])
    vres = None
    if v_mix is not None:
        v0, v1, v2 = v_mix
        extra = extra.at[:, :V_LORA].set(v1)
        vres = (v0, _pad_rows(v2, RW_SMALL, W_LORA + A_LORA + G_LORA), v_first)
    w_pad = jnp.concatenate([w_in[:, :rw_cols], extra, w_in[:, rw_cols:]], axis=1)
    proj3 = _proj(x2, norm, w_pad).reshape(bsz, seq, EVEN_PAD)
    r, k, v, lw, kk, a, g = _rwkv_prep(
        proj3, mu_p, w0, _pad_rows(w2, RW_SMALL, 0), a0, _pad_rows(a2, RW_SMALL, W_LORA),
        _pad_rows(g2, RW_SMALL, W_LORA + A_LORA), k_k, k_a, vres)
    y_rw = _rwkv_chunk(r, k, v, lw, kk, a, g, r_k, ln_w, ln_b)
    y_sb = _sb_attention(proj3, q_gain, k_gain, RW_PAD // LANES)
    n = bsz * seq
    out = _outproj(x2, y_rw.reshape(n, RW_W), y_sb.reshape(n, SB_W), w_out)
    return out, v


def _odd_mixer(x2, bsz, seq, norm, w_in, w_out, s5_params, d_skip, w_glu, b_glu, lb, hg_gain):
    n = bsz * seq
    w_perm = jnp.concatenate([w_in[:, S5_W:], w_in[:, :S5_W]], axis=1)
    proj2 = _proj(x2, norm, w_perm)
    proj3 = proj2.reshape(bsz, seq, ODD_IN)
    tables = _s5_tables(*s5_params, min(S5_CHUNK, seq))
    ssm = _s5_ssm(proj3[..., ODD_IN - S5_W:], tables)
    y_s5 = _s5_post(ssm.reshape(n, S5_W), proj2, d_skip, w_glu, b_glu)
    y_hg = _hgrn2(proj3, lb, hg_gain)
    return _outproj(x2, y_s5, y_hg.reshape(n, HG_VW), w_out)


def kernel(x, ffn1_norm, ffn1_w13, ffn1_w2, mix_norm, ffn2_norm, ffn2_w13, ffn2_w2, ev_w_in, ev_w_out, rw_mu, rw_w0, rw_w2, rw_a0, rw_a2, rw_g2, rw_k_k, rw_k_a, rw_r_k, rw_ln_w, rw_ln_b, rw_v0, rw_v1, rw_v2, sb_q_gain, sb_k_gain, od_w_in, od_w_out, s5_a_re, s5_a_im, s5_b_re, s5_b_im, s5_c_re, s5_c_im, s5_d, s5_log_dt, s5_w_glu, s5_b_glu, hg_lb, hg_gain):
    bsz, seq, d = x.shape
    depth = ffn1_norm.shape[0]
    lb_all = jnp.cumsum(jax.nn.softmax(hg_lb.astype(F32), axis=0), axis=0)
    lb_all = lb_all - lb_all[0]
    x2 = x.reshape(bsz * seq, d)
    v_first = None
    for layer in range(depth):
        x2 = _ffn(x2, ffn1_norm[layer], ffn1_w13[layer], ffn1_w2[layer])
        if layer % 2 == 0:
            e = layer // 2
            v_mix = None if e == 0 else (rw_v0[e - 1], rw_v1[e - 1], rw_v2[e - 1])
            x2, v = _even_mixer(x2, bsz, seq, mix_norm[layer], ev_w_in[e], ev_w_out[e], rw_mu[e], rw_w0[e],
                                rw_w2[e], rw_a0[e], rw_a2[e], rw_g2[e], rw_k_k[e], rw_k_a[e], rw_r_k[e],
                                rw_ln_w[e], rw_ln_b[e], sb_q_gain[e], sb_k_gain[e], v_first, v_mix)
            if e == 0:
                v_first = v
        else:
            o = layer // 2
            s5_params = (s5_a_re[o], s5_a_im[o], s5_b_re[o], s5_b_im[o], s5_c_re[o], s5_c_im[o], s5_log_dt[o])
            x2 = _odd_mixer(x2, bsz, seq, mix_norm[layer], od_w_in[o], od_w_out[o], s5_params, s5_d[o],
                            s5_w_glu[o], s5_b_glu[o], lb_all[o], hg_gain[o])
        x2 = _ffn(x2, ffn2_norm[layer], ffn2_w13[layer], ffn2_w2[layer])
    return x2.reshape(bsz, seq, d)
```

```python
import functools

import jax
import jax.numpy as jnp
from jax import lax
from jax.experimental import pallas as pl
from jax.experimental.pallas import tpu as pltpu

F32 = jnp.float32
BF16 = jnp.bfloat16

LOG2E = 1.4426950408889634
NORM_EPS = 1e-6
GN_EPS = 64e-5
LANES = 128
RW_HEADS, RW_HD, RW_W = 8, 64, 512
W_LORA, A_LORA, V_LORA, G_LORA = 32, 32, 32, 96
RW_SMALL = 256
RW_PAD = 3 * RW_W + RW_SMALL
SB_W, SB_HD = 512, 64
EVEN_PAD = RW_PAD + 3 * SB_W
S5_G, S5_GROUP, S5_W, S5_P = 16, 16, 256, 64
HG_HEADS, HG_DK, HG_KW, HG_VW, HG_BLOCK = 6, 128, 768, 768, 16
ODD_IN = S5_W + 2 * HG_KW + 2 * HG_VW
VMEM_LIMIT = 48 * 1024 * 1024

FFN_TM = 512
FFN_CHUNK = 256
RW_CHUNK = 64
RW_TS = 256
S5_CHUNK = 64
SB_TQ = 512
SB_TK = 128
SB_LANE_TILES = 2
HG_TS = 128


def _cparams(sem):
    return pltpu.CompilerParams(dimension_semantics=sem, vmem_limit_bytes=VMEM_LIMIT)


def _dot(a, b):
    return jnp.dot(a.astype(BF16), b.astype(BF16), preferred_element_type=F32)


def _dot_nt(a, b):
    return lax.dot_general(a.astype(BF16), b.astype(BF16), (((1,), (1,)), ((), ())),
                           preferred_element_type=F32)


def _dot_tn(a, b):
    return lax.dot_general(a.astype(BF16), b.astype(BF16), (((0,), (0,)), ((), ())),
                           preferred_element_type=F32)


def _dot_hi(a, b):
    return jnp.dot(a, b, preferred_element_type=F32, precision=lax.Precision.HIGHEST)


def _split(x):
    hi = x.astype(BF16)
    lo = (x - hi.astype(F32)).astype(BF16)
    return hi, lo


def _dot_sel_right(x, e):
    hi, lo = _split(x)
    return (jnp.dot(hi, e, preferred_element_type=F32) + jnp.dot(lo, e, preferred_element_type=F32))


def _dot_sel_left(e, x):
    hi, lo = _split(x)
    return (jnp.dot(e, hi, preferred_element_type=F32) + jnp.dot(e, lo, preferred_element_type=F32))


def _softplus(x):
    return jnp.maximum(x, 0.0) + jnp.log(1.0 + jnp.exp(-jnp.abs(x)))


def _sigmoid(x):
    return jax.nn.sigmoid(x)


def _iota2(shape, dim):
    return lax.broadcasted_iota(jnp.int32, shape, dim)


def _group_ones(n, group):
    return (_iota2((n, n), 0) // group == _iota2((n, n), 1) // group).astype(BF16)


def _rms_rows(x, gain):
    return x * lax.rsqrt(jnp.mean(x * x, axis=-1, keepdims=True) + NORM_EPS) * gain


def _ffn_kernel(x_ref, g_ref, w13_ref, w2_ref, o_ref, h_scr, act_scr):
    dff = w2_ref.shape[0]
    h_scr[...] = _rms_rows(x_ref[...], g_ref[...]).astype(BF16)
    for j in range(dff // FFN_CHUNK):
        lo = j * FFN_CHUNK
        gate = jnp.dot(h_scr[...], w13_ref[:, lo:lo + FFN_CHUNK], preferred_element_type=F32)
        up = jnp.dot(h_scr[...], w13_ref[:, dff + lo:dff + lo + FFN_CHUNK], preferred_element_type=F32)
        act_scr[:, lo:lo + FFN_CHUNK] = (gate * _sigmoid(gate) * up).astype(BF16)
    o_ref[...] = x_ref[...] + 0.5 * jnp.dot(act_scr[...], w2_ref[...], preferred_element_type=F32)


def _ffn(x2, gain, w13, w2):
    n, d = x2.shape
    dff = w2.shape[0]
    tm = min(FFN_TM, n)
    resident = lambda shape: pl.BlockSpec(shape, lambda i: (0, 0), pipeline_mode=pl.Buffered(1))
    return pl.pallas_call(
        _ffn_kernel,
        out_shape=jax.ShapeDtypeStruct((n, d), F32),
        grid=(n // tm,),
        in_specs=[
            pl.BlockSpec((tm, d), lambda i: (i, 0)),
            resident((1, d)),
            resident((d, 2 * dff)),
            resident((dff, d)),
        ],
        out_specs=pl.BlockSpec((tm, d), lambda i: (i, 0)),
        scratch_shapes=[pltpu.VMEM((tm, d), BF16), pltpu.VMEM((tm, dff), BF16)],
        compiler_params=_cparams(("parallel",)),
        name="ffn",
    )(x2, gain.reshape(1, d), w13.astype(BF16), w2.astype(BF16))


def _proj_kernel(x_ref, g_ref, w_ref, o_ref, h_scr):
    @pl.when(pl.program_id(1) == 0)
    def _():
        h_scr[...] = _rms_rows(x_ref[...], g_ref[...]).astype(BF16)

    o_ref[...] = jnp.dot(h_scr[...], w_ref[...], preferred_element_type=F32)


def _proj(x2, gain, w):
    n, d = x2.shape
    c = w.shape[1]
    tm = min(1024, n)
    tn = c // 2
    return pl.pallas_call(
        _proj_kernel,
        out_shape=jax.ShapeDtypeStruct((n, c), F32),
        grid=(n // tm, c // tn),
        in_specs=[
            pl.BlockSpec((tm, d), lambda i, j: (i, 0)),
            pl.BlockSpec((1, d), lambda i, j: (0, 0)),
            pl.BlockSpec((d, tn), lambda i, j: (0, j)),
        ],
        out_specs=pl.BlockSpec((tm, tn), lambda i, j: (i, j)),
        scratch_shapes=[pltpu.VMEM((tm, d), BF16)],
        compiler_params=_cparams(("parallel", "arbitrary")),
        name="mixer_in_proj",
    )(x2, gain.reshape(1, d), w.astype(BF16))


def _outproj_kernel(x_ref, a1_ref, a2_ref, w1_ref, w2_ref, o_ref):
    o_ref[...] = (x_ref[...]
                  + jnp.dot(a1_ref[...], w1_ref[...], preferred_element_type=F32)
                  + jnp.dot(a2_ref[...], w2_ref[...], preferred_element_type=F32))


def _outproj(x2, a1, a2, w_out):
    n, d = x2.shape
    k1, k2 = a1.shape[1], a2.shape[1]
    tm = min(1024, n)
    w = w_out.astype(BF16)
    return pl.pallas_call(
        _outproj_kernel,
        out_shape=jax.ShapeDtypeStruct((n, d), F32),
        grid=(n // tm,),
        in_specs=[
            pl.BlockSpec((tm, d), lambda i: (i, 0)),
            pl.BlockSpec((tm, k1), lambda i: (i, 0)),
            pl.BlockSpec((tm, k2), lambda i: (i, 0)),
            pl.BlockSpec((k1, d), lambda i: (0, 0)),
            pl.BlockSpec((k2, d), lambda i: (0, 0)),
        ],
        out_specs=pl.BlockSpec((tm, d), lambda i: (i, 0)),
        compiler_params=_cparams(("parallel",)),
        name="mixer_out_proj",
    )(x2, a1, a2, w[:k1], w[k1:])


def _rwkv_prep_kernel(*refs, has_vres):
    if has_vres:
        (p_ref, mu_ref, w0_ref, w2_ref, a0_ref, a2_ref, g2_ref, kk_ref, ka_ref, v0_ref, v2_ref, vf_ref,
         r_o, k_o, v_o, lw_o, kk_o, a_o, g_o, carry) = refs
    else:
        (p_ref, mu_ref, w0_ref, w2_ref, a0_ref, a2_ref, g2_ref, kk_ref, ka_ref,
         r_o, k_o, v_o, lw_o, kk_o, a_o, g_o, carry) = refs
    ts = p_ref.shape[1]

    @pl.when(pl.program_id(1) == 0)
    def _():
        carry[...] = jnp.zeros_like(carry)

    x = p_ref[0]
    prev = pltpu.roll(x, 1, axis=0)
    prev = jnp.where(_iota2((ts, 1), 0) == 0, carry[...], prev)
    carry[...] = x[ts - 1:ts, :]
    rw = x + mu_ref[...] * (prev - x)

    r = rw[:, 0:RW_W]
    k = rw[:, RW_W:2 * RW_W]
    v = rw[:, 2 * RW_W:3 * RW_W]
    small = rw[:, 3 * RW_W:RW_PAD]

    log_w = -_softplus(-(w0_ref[...] + _dot(jnp.tanh(small), w2_ref[...]))) - 0.5
    a = _sigmoid(a0_ref[...] + _dot(small, a2_ref[...]))
    g = _dot(_sigmoid(small), g2_ref[...])
    if has_vres:
        v = v + (vf_ref[0] - v) * _sigmoid(v0_ref[...] + _dot(small, v2_ref[...]))

    kk = k * kk_ref[...]
    ss = _dot_sel_right(kk * kk, _group_ones(RW_W, RW_HD))
    kk = kk * lax.rsqrt(jnp.maximum(ss, 1e-24))

    r_o[0] = r
    k_o[0] = k * (1.0 + (a - 1.0) * ka_ref[...])
    v_o[0] = v
    lw_o[0] = -jnp.exp(log_w)
    kk_o[0] = kk
    a_o[0] = a
    g_o[0] = g


def _rwkv_prep(proj3, mu, w0, w2p, a0, a2p, g2p, k_k, k_a, vres):
    b, s, _ = proj3.shape
    ts = min(512, s)
    has_vres = vres is not None
    row = lambda t: t.reshape(1, -1)
    vec = lambda n: pl.BlockSpec((1, n), lambda i, j: (0, 0))
    lora = pl.BlockSpec((RW_SMALL, RW_W), lambda i, j: (0, 0))
    seq = pl.BlockSpec((1, ts, RW_W), lambda i, j: (i, j, 0))
    args = [proj3, row(mu), row(w0), w2p, row(a0), a2p, g2p, row(k_k), row(k_a)]
    specs = [pl.BlockSpec((1, ts, RW_PAD), lambda i, j: (i, j, 0)), vec(RW_PAD), vec(RW_W), lora,
             vec(RW_W), lora, lora, vec(RW_W), vec(RW_W)]
    if has_vres:
        v0, v2p, v_first = vres
        args += [row(v0), v2p, v_first]
        specs += [vec(RW_W), lora, seq]
    out = jax.ShapeDtypeStruct((b, s, RW_W), F32)
    return pl.pallas_call(
        functools.partial(_rwkv_prep_kernel, has_vres=has_vres),
        out_shape=[out] * 7,
        grid=(b, s // ts),
        in_specs=specs,
        out_specs=[seq] * 7,
        scratch_shapes=[pltpu.VMEM((1, RW_PAD), F32)],
        compiler_params=_cparams(("parallel", "arbitrary")),
        name="rwkv_prep",
    )(*args)


def _dot3(a, b):
    ah, al = a if isinstance(a, tuple) else _split(a)
    bh, bl = b if isinstance(b, tuple) else _split(b)
    d = lambda x, y: jnp.dot(x, y, preferred_element_type=F32)
    return d(ah, bh) + (d(ah, bl) + d(al, bh))


def _unit_lower_inverses(ns, l):
    size = ns[0].shape[0]
    row, col = _iota2((size, size), 0), _iota2((size, size), 1)
    eye = (row == col).astype(F32)
    same = lambda blk: (row // blk) == (col // blk)
    n1 = [_split(jnp.where(same(8), n, 0.0)) for n in ns]
    n2 = [_dot3(a, a) for a in n1]
    n2s = [_split(a) for a in n2]
    n4 = [_dot3(a, a) for a in n2s]
    n3 = [_dot3(a, b) for a, b in zip(n1, n2s)]
    p1 = [eye + jnp.where(same(8), n, 0.0) + b + c for n, b, c in zip(ns, n2, n3)]
    t = [p + _dot3(p, q) for p, q in zip(p1, n4)]
    blk = 8
    while blk < l:
        level = same(2 * blk) & jnp.logical_not(same(blk))
        tb = [a.astype(BF16) for a in t]
        tn = [_dot(a, jnp.where(level, n, 0.0)) for a, n in zip(tb, ns)]
        t = [a + _dot(b, c) for a, b, c in zip(t, tn, tb)]
        blk *= 2
    return t


def _rwkv_chunk_kernel(r_ref, k_ref, v_ref, lw_ref, kk_ref, a_ref, g_ref, rk_ref, lnw_ref, lnb_ref,
                       o_ref, h_scr):
    l = RW_CHUNK

    @pl.when(pl.program_id(1) == 0)
    def _():
        h_scr[...] = jnp.zeros_like(h_scr)

    pairs = range(RW_W // LANES)
    n_chunks = r_ref.shape[1] // l
    units = [(c, p) for c in range(n_chunks) for p in pairs]
    rws = [slice(c * l, (c + 1) * l) for c, _ in units]
    sls = [slice(p * LANES, (p + 1) * LANES) for _, p in units]
    tri_incl = (_iota2((l, l), 1) <= _iota2((l, l), 0)).astype(BF16)
    m0 = _iota2((1, LANES), 1) < RW_HD
    rr, cc = _iota2((2 * l, 2 * l), 0), _iota2((2 * l, 2 * l), 1)
    same_head = (rr // l) == (cc // l)
    strict = same_head & ((rr % l) > (cc % l))
    incl = same_head & ((rr % l) >= (cc % l))
    eye = _iota2((LANES, LANES), 0) == _iota2((LANES, LANES), 1)
    head_ones = _group_ones(LANES, RW_HD)
    stack = lambda t: jnp.concatenate([jnp.where(m0, t, 0.0), jnp.where(m0, 0.0, t)], axis=0)
    unstack = lambda t: t[:l] + t[l:]

    load = lambda ref: [ref[0, rw, s] for rw, s in zip(rws, sls)]
    r, k, v, lw, kk_n, a_lr = load(r_ref), load(k_ref), load(v_ref), load(lw_ref), load(kk_ref), load(a_ref)
    gc = [_dot_sel_left(tri_incl, x) for x in lw]
    g_last = [g[l - 1:l, :] for g in gc]
    e_neg = [jnp.exp(-g) for g in gc]
    e_end = [jnp.exp(gl - g) for gl, g in zip(g_last, gc)]
    bv = [x * y for x, y in zip(kk_n, a_lr)]
    a_s = [stack(-x * jnp.exp(g - w)) for x, g, w in zip(kk_n, gc, lw)]
    r_s = [stack(x * jnp.exp(g)) for x, g in zip(r, gc)]
    b_s = [stack(x * e) for x, e in zip(bv, e_neg)]
    k_s = [stack(x * e) for x, e in zip(k, e_neg)]
    v_s = [stack(x) for x in v]
    end_s = [jnp.concatenate([stack(x * e), stack(y * e)], axis=0) for x, y, e in zip(bv, k, e_end)]

    prod = [_dot_nt(jnp.concatenate([a, rq], axis=0), jnp.concatenate([b, kq], axis=0))
            for a, rq, b, kq in zip(a_s, r_s, b_s, k_s)]
    t_inv = _unit_lower_inverses([jnp.where(strict, x[:2 * l, :2 * l], 0.0) for x in prod], l)
    x_s = [_dot(jnp.where(strict, pr[:2 * l, 2 * l:], 0.0), vs) for pr, vs in zip(prod, v_s)]
    wu = [_dot(t, jnp.concatenate([a, x], axis=1)) for t, a, x in zip(t_inv, a_s, x_s)]
    big = [jnp.concatenate([w, jnp.concatenate([jnp.zeros_like(vs), vs], axis=1)], axis=0)
           for w, vs in zip(wu, v_s)]
    r_bk = [jnp.concatenate([jnp.where(incl, pr[2 * l:, :2 * l], 0.0), jnp.where(incl, pr[2 * l:, 2 * l:], 0.0)],
                            axis=1) for pr in prod]
    qy = [_dot(x, b) for x, b in zip(r_bk, big)]
    mn = [_dot_tn(e, b) for e, b in zip(end_s, big)]

    q_s = [rq + x[:, :LANES] for rq, x in zip(r_s, qy)]
    m_mat = [x[:, :LANES] + jnp.where(eye, jnp.broadcast_to(jnp.exp(gl), (LANES, LANES)), 0.0)
             for x, gl in zip(mn, g_last)]
    state = [h_scr[p] for p in pairs]
    y = []
    for c in range(n_chunks):
        u0 = c * len(pairs)
        y += [unstack(_dot(q_s[u0 + p], state[p]) + qy[u0 + p][:, LANES:]) for p in pairs]
        state = [_dot3(m_mat[u0 + p], state[p]) + mn[u0 + p][:, LANES:] for p in pairs]
    for p in pairs:
        h_scr[p] = state[p]

    for u, (rw, sl) in enumerate(zip(rws, sls)):
        mean = _dot_sel_right(y[u], head_ones) * (1.0 / RW_HD)
        yc = y[u] - mean
        var = _dot_sel_right(yc * yc, head_ones) * (1.0 / RW_HD)
        yn = yc * lax.rsqrt(var + GN_EPS) * lnw_ref[:, sl] + lnb_ref[:, sl]
        bonus = _dot_sel_right(r[u] * k[u] * rk_ref[:, sl], head_ones) * v[u]
        o_ref[0, rw, sl] = ((yn + bonus) * g_ref[0, rw, sl]).astype(o_ref.dtype)


def _rwkv_chunk(r, k, v, lw, kk, a, g, r_k, ln_w, ln_b):
    b, s, _ = r.shape
    ts = min(RW_TS, s)
    seq = pl.BlockSpec((1, ts, RW_W), lambda i, j: (i, j, 0))
    vec = pl.BlockSpec((1, RW_W), lambda i, j: (0, 0))
    return pl.pallas_call(
        _rwkv_chunk_kernel,
        out_shape=jax.ShapeDtypeStruct((b, s, RW_W), BF16),
        grid=(b, s // ts),
        in_specs=[seq] * 7 + [vec] * 3,
        out_specs=seq,
        scratch_shapes=[pltpu.VMEM((RW_W // LANES, LANES, LANES), F32)],
        compiler_params=_cparams(("parallel", "arbitrary")),
        name="rwkv_chunk",
    )(r, k, v, lw, kk, a, g, r_k.reshape(1, -1), ln_w.reshape(1, -1), ln_b.reshape(1, -1))


def _pair_rmsnorm(t, gain_row, ones):
    ms = _dot_sel_right(t * t, ones) * (1.0 / SB_HD)
    return t * lax.rsqrt(ms + NORM_EPS) * gain_row


def _sb_kernel(q_ref, k_ref, v_ref, qg_ref, kg_ref, o_ref, kn_scr, vb_scr, qs_scr, c_scr, acc_scr):
    tq = q_ref.shape[1]
    tk = SB_TK
    n_tiles = q_ref.shape[2] // LANES
    qb = pl.program_id(2)
    ones = _group_ones(LANES, SB_HD)
    lanes = [slice(t * LANES, (t + 1) * LANES) for t in range(n_tiles)]

    @pl.when(qb == 0)
    def _():
        for sl in lanes:
            kn_scr[:, sl] = _pair_rmsnorm(k_ref[0, :, sl], kg_ref[:, sl], ones).astype(BF16)
        vb_scr[...] = v_ref[0].astype(BF16)

    m0 = _iota2((1, LANES), 1) < SB_HD
    for t, sl in enumerate(lanes):
        q = _pair_rmsnorm(q_ref[0, :, sl], qg_ref[:, sl], ones) * (SB_HD ** -0.5)
        qs_scr[(2 * t) * tq:(2 * t + 1) * tq, :] = jnp.where(m0, q, 0.0).astype(BF16)
        qs_scr[(2 * t + 1) * tq:(2 * t + 2) * tq, :] = jnp.where(m0, 0.0, q).astype(BF16)
    c_scr[...] = jnp.zeros_like(c_scr)
    acc_scr[...] = jnp.zeros_like(acc_scr)

    tri_after = (_iota2((2 * tk, 2 * tk), 0) > _iota2((2 * tk, 2 * tk), 1)).astype(BF16)

    def tile_pair(j_old, masked):
        start = pl.multiple_of(j_old * tk, tk)
        if masked:
            q_pos = qb * tq + _iota2((tq, 2 * tk), 0)
            before = (start + _iota2((tq, 2 * tk), 1)) < q_pos
        for t, sl in enumerate(lanes):
            kk = kn_scr[pl.ds(start, 2 * tk), sl]
            vv = vb_scr[pl.ds(start, 2 * tk), sl]
            for h in range(2):
                rows = slice((2 * t + h) * tq, (2 * t + h + 1) * tq)
                z = lax.dot_general(qs_scr[rows, :], kk, (((1,), (1,)), ((), ())), preferred_element_type=F32)
                drop = jnp.maximum(z, 0.0) + jnp.log(1.0 + jnp.exp2(jnp.abs(z) * (-LOG2E)))
                if masked:
                    drop = jnp.where(before, drop, 0.0)
                after = jnp.dot(drop.astype(BF16), tri_after, preferred_element_type=F32)
                c = c_scr[rows, :]
                wgt = jnp.exp(z - (drop + after + c))
                if masked:
                    wgt = jnp.where(before, wgt, 0.0)
                c_scr[rows, :] = c + jnp.sum(drop, axis=1, keepdims=True)
                acc_scr[rows, :] += jnp.dot(wgt.astype(BF16), vv, preferred_element_type=F32)

    tiles_per_block = tq // tk
    for i in reversed(range(tiles_per_block // 2)):
        tile_pair(qb * tiles_per_block + 2 * i, True)

    def body(it, carry):
        tile_pair(qb * tiles_per_block - 2 - 2 * it, False)
        return carry

    lax.fori_loop(0, qb * (tiles_per_block // 2), body, 0)
    for t, sl in enumerate(lanes):
        o_ref[0, :, sl] = jnp.where(m0, acc_scr[(2 * t) * tq:(2 * t + 1) * tq, :],
                                    acc_scr[(2 * t + 1) * tq:(2 * t + 2) * tq, :]).astype(o_ref.dtype)


def _sb_attention(proj3, q_gain, k_gain, col0):
    b, s, _ = proj3.shape
    tq = min(SB_TQ, s)
    assert tq % (2 * SB_TK) == 0 and s % tq == 0
    w = SB_LANE_TILES * LANES
    nblk = SB_W // w
    c0 = col0 // SB_LANE_TILES
    gains = lambda g: jnp.tile(g, w // SB_HD).reshape(1, w)
    rows = SB_LANE_TILES * 2 * tq
    return pl.pallas_call(
        _sb_kernel,
        out_shape=jax.ShapeDtypeStruct((b, s, SB_W), BF16),
        grid=(b, nblk, s // tq),
        in_specs=[
            pl.BlockSpec((1, tq, w), lambda i, p, j: (i, j, c0 + p)),
            pl.BlockSpec((1, s, w), lambda i, p, j: (i, 0, c0 + nblk + p)),
            pl.BlockSpec((1, s, w), lambda i, p, j: (i, 0, c0 + 2 * nblk + p)),
            pl.BlockSpec((1, w), lambda i, p, j: (0, 0)),
            pl.BlockSpec((1, w), lambda i, p, j: (0, 0)),
        ],
        out_specs=pl.BlockSpec((1, tq, w), lambda i, p, j: (i, j, p)),
        scratch_shapes=[pltpu.VMEM((s, w), BF16), pltpu.VMEM((s, w), BF16),
                        pltpu.VMEM((rows, LANES), BF16), pltpu.VMEM((rows, 1), F32),
                        pltpu.VMEM((rows, LANES), F32)],
        compiler_params=_cparams(("parallel", "parallel", "arbitrary")),
        name="stick_breaking",
    )(proj3, proj3, proj3, gains(q_gain), gains(k_gain))


def _s5_tables(a_re, a_im, b_re, b_im, c_re, c_im, log_dt, l):
    lam = lax.complex(jnp.minimum(a_re.astype(F32), -1e-4), a_im.astype(F32))
    lam_dt = lam * jnp.exp(log_dt.astype(F32))[:, None]
    lam_bar = jnp.exp(lam_dt)
    b_bar = ((lam_bar - 1.0) / lam)[..., None] * lax.complex(b_re.astype(F32), b_im.astype(F32))
    c = lax.complex(c_re.astype(F32), c_im.astype(F32))
    steps = jnp.arange(l + 1, dtype=F32)
    powers = jnp.exp(lam_dt[:, None, :] * steps[None, :, None])
    g, p = lam.shape
    kern = jnp.einsum('gcp,gjp,gpd->gjcd', c, powers[:, :l], b_bar).real
    lag = jnp.arange(l)[None, :] - jnp.arange(l)[:, None]
    toep = jnp.where((lag >= 0)[None, :, :, None, None], kern[:, jnp.clip(lag, 0, l - 1)], 0.0)
    toep = toep.transpose(0, 1, 4, 2, 3).reshape(g, l * S5_GROUP, l * S5_GROUP)
    b_end = powers[:, :l][:, ::-1, :, None] * b_bar[:, None, :, :]
    b_end = b_end.transpose(0, 1, 3, 2).reshape(g, l * S5_GROUP, p)
    b_end = jnp.concatenate([b_end.real, b_end.imag], axis=-1)
    c_pow = c[:, None, :, :] * powers[:, 1:, None, :]
    c_pow = jnp.concatenate([c_pow.real, -c_pow.imag], axis=-1)
    c_pow = c_pow.transpose(0, 3, 1, 2).reshape(g, 2 * p, l * S5_GROUP)
    lam_l = powers[:, l]
    mul_same = jnp.concatenate([lam_l.real, lam_l.real], axis=-1).reshape(g, 1, 2 * p)
    mul_swap = jnp.concatenate([-lam_l.imag, lam_l.imag], axis=-1).reshape(g, 1, 2 * p)
    return toep.astype(BF16), b_end.astype(BF16), c_pow.astype(BF16), mul_same, mul_swap


def _s5_kernel(u_ref, toep_ref, bend_ref, cpow_ref, same_ref, swap_ref, y_ref, e_scr, x_scr, *, bsz):
    u = u_ref[0].astype(BF16)
    e_scr[...] = jnp.dot(u, bend_ref[0], preferred_element_type=F32)
    n_chunks = u.shape[0] // bsz
    same, swap = same_ref[0], swap_ref[0]

    def body(c, x):
        rows = pl.ds(pl.multiple_of(c * bsz, bsz), bsz)
        x_scr[rows, :] = x
        return x * same + pltpu.roll(x, S5_P, axis=1) * swap + e_scr[rows, :]

    lax.fori_loop(0, n_chunks, body, jnp.zeros((bsz, 2 * S5_P), F32))
    y_ref[0] = (jnp.dot(u, toep_ref[0], preferred_element_type=F32)
                + jnp.dot(x_scr[...].astype(BF16), cpow_ref[0], preferred_element_type=F32))


def _s5_ssm(u, tables):
    b, s, _ = u.shape
    l = min(S5_CHUNK, s)
    nc = s // l
    toep, b_end, c_pow, mul_same, mul_swap = tables
    width = l * S5_GROUP
    ug = u.reshape(b, nc, l, S5_G, S5_GROUP).transpose(3, 1, 0, 2, 4).reshape(S5_G, nc * b, width)
    grp = lambda shape: pl.BlockSpec((1,) + shape, lambda g: (g, 0, 0))
    y = pl.pallas_call(
        functools.partial(_s5_kernel, bsz=b),
        out_shape=jax.ShapeDtypeStruct((S5_G, nc * b, width), F32),
        grid=(S5_G,),
        in_specs=[grp((nc * b, width)), grp((width, width)), grp((width, 2 * S5_P)),
                  grp((2 * S5_P, width)), grp((1, 2 * S5_P)), grp((1, 2 * S5_P))],
        out_specs=grp((nc * b, width)),
        scratch_shapes=[pltpu.VMEM((nc * b, 2 * S5_P), F32), pltpu.VMEM((nc * b, 2 * S5_P), F32)],
        compiler_params=_cparams(("parallel",)),
        name="s5_ssm",
    )(ug, toep, b_end, c_pow, mul_same, mul_swap)
    return y.reshape(S5_G, nc, b, l, S5_GROUP).transpose(2, 1, 3, 0, 4).reshape(b, s, S5_W)


def _s5_post_kernel(y_ref, u_ref, d_ref, w_ref, b_ref, o_ref):
    y = y_ref[...] + d_ref[...] * u_ref[...]
    y = 0.5 * y * (1.0 + jnp.tanh(0.7978845608028654 * (y + 0.044715 * (y * y * y))))
    o_ref[...] = (y * _sigmoid(_dot(y, w_ref[...]) + b_ref[...])).astype(o_ref.dtype)


def _s5_post(y2, proj2, d, w_glu, b_glu):
    n = y2.shape[0]
    tm = min(2048, n)
    vec = pl.BlockSpec((1, S5_W), lambda i: (0, 0))
    return pl.pallas_call(
        _s5_post_kernel,
        out_shape=jax.ShapeDtypeStruct((n, S5_W), BF16),
        grid=(n // tm,),
        in_specs=[pl.BlockSpec((tm, S5_W), lambda i: (i, 0)),
                  pl.BlockSpec((tm, S5_W), lambda i: (i, ODD_IN // S5_W - 1)),
                  vec, pl.BlockSpec((S5_W, S5_W), lambda i: (0, 0)), vec],
        out_specs=pl.BlockSpec((tm, S5_W), lambda i: (i, 0)),
        compiler_params=_cparams(("parallel",)),
        name="s5_glu",
    )(y2, proj2, d.reshape(1, -1), w_glu.astype(BF16), b_glu.reshape(1, -1))


def _hgrn2_kernel(q_ref, f_ref, i_ref, gate_ref, loglb_ref, log1mlb_ref, onemlb_ref, gain_ref,
                  o_ref, st_scr):
    ts = q_ref.shape[1]
    lb = HG_BLOCK
    heads = range(HG_HEADS)
    sls = [slice(h * HG_DK, (h + 1) * HG_DK) for h in heads]

    @pl.when(pl.program_id(1) == 0)
    def _():
        st_scr[...] = jnp.zeros_like(st_scr)

    rr, cc = _iota2((ts, ts), 0), _iota2((ts, ts), 1)
    same = (rr // lb) == (cc // lb)
    causal = same & (cc <= rr)
    tri, blk = causal.astype(BF16), same.astype(BF16)
    sel = jnp.concatenate([jnp.concatenate([tri, tri], axis=1), jnp.concatenate([blk, blk], axis=1)], axis=0)

    fl = f_ref[0]
    hi_arg = loglb_ref[...]
    lo_arg = log1mlb_ref[...] - _softplus(-fl)
    log_f = jnp.maximum(hi_arg, lo_arg) + jnp.log(1.0 + jnp.exp(-jnp.abs(hi_arg - lo_arg)))
    kk = onemlb_ref[...] * _sigmoid(-fl)
    hi, lo = _split(log_f)
    gg = jnp.dot(sel, jnp.concatenate([hi, lo], axis=0), preferred_element_type=F32)
    g_cum, g_tot = gg[:ts], gg[ts:]
    q_in = q_ref[0] * jnp.exp(g_cum)
    k_in = kk * jnp.exp(-g_cum)
    k_end = kk * jnp.exp(g_tot - g_cum)
    decay = jnp.exp(g_tot)
    val = i_ref[0]
    att = [jnp.where(causal, _dot_nt(q_in[:, s], k_in[:, s]), 0.0) for s in sls]
    intra = [_dot(a, val[:, s]) for a, s in zip(att, sls)]
    n_sub = ts // lb
    rows = [slice(c * lb, (c + 1) * lb) for c in range(n_sub)]
    push = [[_dot_tn(val[r, s], k_end[r, s]) for s in sls] for r in rows]

    st = [st_scr[h] for h in heads]
    inter = []
    for c, r in enumerate(rows):
        inter.append([_dot_nt(q_in[r, s], x) for s, x in zip(sls, st)])
        st = [x * decay[c * lb:c * lb + 1, s] + p for x, s, p in zip(st, sls, push[c])]
    for h in heads:
        st_scr[h] = st[h]

    gate = gate_ref[0]
    for h, s in zip(heads, sls):
        o = intra[h] + jnp.concatenate([inter[c][h] for c in range(n_sub)], axis=0)
        o = o * lax.rsqrt(jnp.mean(o * o, axis=-1, keepdims=True) + NORM_EPS) * gain_ref[:, s]
        o_ref[0, :, s] = (o * (gate[:, s] * _sigmoid(gate[:, s]))).astype(o_ref.dtype)


def _hgrn2(proj3, lb, gain):
    b, s, _ = proj3.shape
    ts = min(HG_TS, s)
    w = HG_KW
    lb = lb.astype(F32).reshape(1, w)
    col = lambda c: pl.BlockSpec((1, ts, w), lambda i, j: (i, j, c))
    vec = pl.BlockSpec((1, w), lambda i, j: (0, 0))
    return pl.pallas_call(
        _hgrn2_kernel,
        out_shape=jax.ShapeDtypeStruct((b, s, w), BF16),
        grid=(b, s // ts),
        in_specs=[col(0), col(1), col(2), col(3)] + [vec] * 4,
        out_specs=col(0),
        scratch_shapes=[pltpu.VMEM((HG_HEADS, HG_DK, HG_DK), F32)],
        compiler_params=_cparams(("parallel", "arbitrary")),
        name="hgrn2",
    )(proj3, proj3, proj3, proj3, jnp.log(lb), jnp.log1p(-lb), 1.0 - lb, gain.reshape(1, w))


def _pad_rows(w, rows, start):
    return jnp.zeros((rows, w.shape[1]), BF16).at[start:start + w.shape[0]].set(w.astype(BF16))


def _even_mixer(x2, bsz, seq, norm, w_in, w_out, mu, w0, w2, a0, a2, g2, k_k, k_a, r_k, ln_w, ln_b,
                q_gain, k_gain, v_first, v_mix):
    d = x2.shape[1]
    rw_cols = 3 * RW_W + W_LORA + A_LORA + G_LORA
    pad = RW_PAD - rw_cols
    extra = jnp.zeros((d, pad), w_in.dtype)
    mu_p = jnp.concatenate([mu, jnp.zeros((pad,), mu.dtype)])
    vres = None
    if v_mix is not None:
        v0, v1, v2 = v_mix
        extra = extra.at[:, :V_LORA].set(v1)
        vres = (v0, _pad_rows(v2, RW_SMALL, W_LORA + A_LORA + G_LORA), v_first)
    w_pad = jnp.concatenate([w_in[:, :rw_cols], extra, w_in[:, rw_cols:]], axis=1)
    proj3 = _proj(x2, norm, w_pad).reshape(bsz, seq, EVEN_PAD)
    r, k, v, lw, kk, a, g = _rwkv_prep(
        proj3, mu_p, w0, _pad_rows(w2, RW_SMALL, 0), a0, _pad_rows(a2, RW_SMALL, W_LORA),
        _pad_rows(g2, RW_SMALL, W_LORA + A_LORA), k_k, k_a, vres)
    y_rw = _rwkv_chunk(r, k, v, lw, kk, a, g, r_k, ln_w, ln_b)
    y_sb = _sb_attention(proj3, q_gain, k_gain, RW_PAD // LANES)
    n = bsz * seq
    out = _outproj(x2, y_rw.reshape(n, RW_W), y_sb.reshape(n, SB_W), w_out)
    return out, v


def _odd_mixer(x2, bsz, seq, norm, w_in, w_out, s5_params, d_skip, w_glu, b_glu, lb, hg_gain):
    n = bsz * seq
    w_perm = jnp.concatenate([w_in[:, S5_W:], w_in[:, :S5_W]], axis=1)
    proj2 = _proj(x2, norm, w_perm)
    proj3 = proj2.reshape(bsz, seq, ODD_IN)
    tables = _s5_tables(*s5_params, min(S5_CHUNK, seq))
    ssm = _s5_ssm(proj3[..., ODD_IN - S5_W:], tables)
    y_s5 = _s5_post(ssm.reshape(n, S5_W), proj2, d_skip, w_glu, b_glu)
    y_hg = _hgrn2(proj3, lb, hg_gain)
    return _outproj(x2, y_s5, y_hg.reshape(n, HG_VW), w_out)


def kernel(x, ffn1_norm, ffn1_w13, ffn1_w2, mix_norm, ffn2_norm, ffn2_w13, ffn2_w2, ev_w_in, ev_w_out, rw_mu, rw_w0, rw_w2, rw_a0, rw_a2, rw_g2, rw_k_k, rw_k_a, rw_r_k, rw_ln_w, rw_ln_b, rw_v0, rw_v1, rw_v2, sb_q_gain, sb_k_gain, od_w_in, od_w_out, s5_a_re, s5_a_im, s5_b_re, s5_b_im, s5_c_re, s5_c_im, s5_d, s5_log_dt, s5_w_glu, s5_b_glu, hg_lb, hg_gain):
    bsz, seq, d = x.shape
    depth = ffn1_norm.shape[0]
    lb_all = jnp.cumsum(jax.nn.softmax(hg_lb.astype(F32), axis=0), axis=0)
    lb_all = lb_all - lb_all[0]
    x2 = x.reshape(bsz * seq, d)
    v_first = None
    for layer in range(depth):
        x2 = _ffn(x2, ffn1_norm[layer], ffn1_w13[layer], ffn1_w2[layer])
        if layer % 2 == 0:
            e = layer // 2
            v_mix = None if e == 0 else (rw_v0[e - 1], rw_v1[e - 1], rw_v2[e - 1])
            x2, v = _even_mixer(x2, bsz, seq, mix_norm[layer], ev_w_in[e], ev_w_out[e], rw_mu[e], rw_w0[e],
                                rw_w2[e], rw_a0[e], rw_a2[e], rw_g2[e], rw_k_k[e], rw_k_a[e], rw_r_k[e],
                                rw_ln_w[e], rw_ln_b[e], sb_q_gain[e], sb_k_gain[e], v_first, v_mix)
            if e == 0:
                v_first = v
        else:
            o = layer // 2
            s5_params = (s5_a_re[o], s5_a_im[o], s5_b_re[o], s5_b_im[o], s5_c_re[o], s5_c_im[o], s5_log_dt[o])
            x2 = _odd_mixer(x2, bsz, seq, mix_norm[layer], od_w_in[o], od_w_out[o], s5_params, s5_d[o],
                            s5_w_glu[o], s5_b_glu[o], lb_all[o], hg_gain[o])
        x2 = _ffn(x2, ffn2_norm[layer], ffn2_w13[layer], ffn2_w2[layer])
    return x2.reshape(bsz, seq, d)
```

```python
import functools

import jax
import jax.numpy as jnp
from jax import lax
from jax.experimental import pallas as pl
from jax.experimental.pallas import tpu as pltpu

F32 = jnp.float32
BF16 = jnp.bfloat16

LOG2E = 1.4426950408889634
NORM_EPS = 1e-6
GN_EPS = 64e-5
LANES = 128
RW_HEADS, RW_HD, RW_W = 8, 64, 512
W_LORA, A_LORA, V_LORA, G_LORA = 32, 32, 32, 96
RW_SMALL = 256
RW_PAD = 3 * RW_W + RW_SMALL
SB_W, SB_HD = 512, 64
SB_COL0 = 2048
EVEN_PAD = SB_COL0 + 3 * SB_W
S5_G, S5_GROUP, S5_W, S5_P = 16, 16, 256, 64
HG_HEADS, HG_DK, HG_KW, HG_VW, HG_BLOCK = 6, 128, 768, 768, 16
ODD_IN = S5_W + 2 * HG_KW + 2 * HG_VW
VMEM_LIMIT = 48 * 1024 * 1024

FFN_TM = 512
FFN_CHUNK = 256
RW_CHUNK = 64
RW_TS = 256
S5_CHUNK = 64
SB_TQ = 512
SB_TK = 128
SB_LANE_TILES = 4
HG_TS = 128


def _cparams(sem):
    return pltpu.CompilerParams(dimension_semantics=sem, vmem_limit_bytes=VMEM_LIMIT)


def _dot(a, b):
    return jnp.dot(a.astype(BF16), b.astype(BF16), preferred_element_type=F32)


def _dot_nt(a, b):
    return lax.dot_general(a.astype(BF16), b.astype(BF16), (((1,), (1,)), ((), ())),
                           preferred_element_type=F32)


def _dot_tn(a, b):
    return lax.dot_general(a.astype(BF16), b.astype(BF16), (((0,), (0,)), ((), ())),
                           preferred_element_type=F32)


def _dot_hi(a, b):
    return jnp.dot(a, b, preferred_element_type=F32, precision=lax.Precision.HIGHEST)


def _split(x):
    hi = x.astype(BF16)
    lo = (x - hi.astype(F32)).astype(BF16)
    return hi, lo


def _dot_sel_right(x, e):
    hi, lo = _split(x)
    return (jnp.dot(hi, e, preferred_element_type=F32) + jnp.dot(lo, e, preferred_element_type=F32))


def _dot_sel_left(e, x):
    hi, lo = _split(x)
    return (jnp.dot(e, hi, preferred_element_type=F32) + jnp.dot(e, lo, preferred_element_type=F32))


def _softplus(x):
    return jnp.maximum(x, 0.0) + jnp.log(1.0 + jnp.exp(-jnp.abs(x)))


def _sigmoid(x):
    return jax.nn.sigmoid(x)


def _iota2(shape, dim):
    return lax.broadcasted_iota(jnp.int32, shape, dim)


def _group_ones(n, group):
    return (_iota2((n, n), 0) // group == _iota2((n, n), 1) // group).astype(BF16)


def _rms_rows(x, gain):
    return x * lax.rsqrt(jnp.mean(x * x, axis=-1, keepdims=True) + NORM_EPS) * gain


def _ffn_kernel(x_ref, g_ref, w13_ref, w2_ref, o_ref, h_scr, act_scr):
    dff = w2_ref.shape[0]
    h_scr[...] = _rms_rows(x_ref[...], g_ref[...]).astype(BF16)
    for j in range(dff // FFN_CHUNK):
        lo = j * FFN_CHUNK
        gate = jnp.dot(h_scr[...], w13_ref[:, lo:lo + FFN_CHUNK], preferred_element_type=F32)
        up = jnp.dot(h_scr[...], w13_ref[:, dff + lo:dff + lo + FFN_CHUNK], preferred_element_type=F32)
        act_scr[:, lo:lo + FFN_CHUNK] = (gate * _sigmoid(gate) * up).astype(BF16)
    o_ref[...] = x_ref[...] + 0.5 * jnp.dot(act_scr[...], w2_ref[...], preferred_element_type=F32)


def _ffn(x2, gain, w13, w2):
    n, d = x2.shape
    dff = w2.shape[0]
    tm = min(FFN_TM, n)
    resident = lambda shape: pl.BlockSpec(shape, lambda i: (0, 0), pipeline_mode=pl.Buffered(1))
    return pl.pallas_call(
        _ffn_kernel,
        out_shape=jax.ShapeDtypeStruct((n, d), F32),
        grid=(n // tm,),
        in_specs=[
            pl.BlockSpec((tm, d), lambda i: (i, 0)),
            resident((1, d)),
            resident((d, 2 * dff)),
            resident((dff, d)),
        ],
        out_specs=pl.BlockSpec((tm, d), lambda i: (i, 0)),
        scratch_shapes=[pltpu.VMEM((tm, d), BF16), pltpu.VMEM((tm, dff), BF16)],
        compiler_params=_cparams(("parallel",)),
        name="ffn",
    )(x2, gain.reshape(1, d), w13.astype(BF16), w2.astype(BF16))


def _proj_kernel(x_ref, g_ref, w_ref, o_ref, h_scr):
    @pl.when(pl.program_id(1) == 0)
    def _():
        h_scr[...] = _rms_rows(x_ref[...], g_ref[...]).astype(BF16)

    o_ref[...] = jnp.dot(h_scr[...], w_ref[...], preferred_element_type=F32)


def _proj(x2, gain, w):
    n, d = x2.shape
    c = w.shape[1]
    tm = min(1024, n)
    tn = c // 2
    return pl.pallas_call(
        _proj_kernel,
        out_shape=jax.ShapeDtypeStruct((n, c), F32),
        grid=(n // tm, c // tn),
        in_specs=[
            pl.BlockSpec((tm, d), lambda i, j: (i, 0)),
            pl.BlockSpec((1, d), lambda i, j: (0, 0)),
            pl.BlockSpec((d, tn), lambda i, j: (0, j)),
        ],
        out_specs=pl.BlockSpec((tm, tn), lambda i, j: (i, j)),
        scratch_shapes=[pltpu.VMEM((tm, d), BF16)],
        compiler_params=_cparams(("parallel", "arbitrary")),
        name="mixer_in_proj",
    )(x2, gain.reshape(1, d), w.astype(BF16))


def _outproj_kernel(x_ref, a1_ref, a2_ref, w1_ref, w2_ref, o_ref):
    o_ref[...] = (x_ref[...]
                  + jnp.dot(a1_ref[...], w1_ref[...], preferred_element_type=F32)
                  + jnp.dot(a2_ref[...], w2_ref[...], preferred_element_type=F32))


def _outproj(x2, a1, a2, w_out):
    n, d = x2.shape
    k1, k2 = a1.shape[1], a2.shape[1]
    tm = min(1024, n)
    w = w_out.astype(BF16)
    return pl.pallas_call(
        _outproj_kernel,
        out_shape=jax.ShapeDtypeStruct((n, d), F32),
        grid=(n // tm,),
        in_specs=[
            pl.BlockSpec((tm, d), lambda i: (i, 0)),
            pl.BlockSpec((tm, k1), lambda i: (i, 0)),
            pl.BlockSpec((tm, k2), lambda i: (i, 0)),
            pl.BlockSpec((k1, d), lambda i: (0, 0)),
            pl.BlockSpec((k2, d), lambda i: (0, 0)),
        ],
        out_specs=pl.BlockSpec((tm, d), lambda i: (i, 0)),
        compiler_params=_cparams(("parallel",)),
        name="mixer_out_proj",
    )(x2, a1, a2, w[:k1], w[k1:])


def _rwkv_prep_kernel(*refs, has_vres):
    if has_vres:
        (p_ref, mu_ref, w0_ref, w2_ref, a0_ref, a2_ref, g2_ref, kk_ref, ka_ref, v0_ref, v2_ref, vf_ref,
         r_o, k_o, v_o, lw_o, kk_o, a_o, g_o, carry) = refs
    else:
        (p_ref, mu_ref, w0_ref, w2_ref, a0_ref, a2_ref, g2_ref, kk_ref, ka_ref,
         r_o, k_o, v_o, lw_o, kk_o, a_o, g_o, carry) = refs
    ts = p_ref.shape[1]

    @pl.when(pl.program_id(1) == 0)
    def _():
        carry[...] = jnp.zeros_like(carry)

    x = p_ref[0]
    prev = pltpu.roll(x, 1, axis=0)
    prev = jnp.where(_iota2((ts, 1), 0) == 0, carry[...], prev)
    carry[...] = x[ts - 1:ts, :]
    rw = x + mu_ref[...] * (prev - x)

    r = rw[:, 0:RW_W]
    k = rw[:, RW_W:2 * RW_W]
    v = rw[:, 2 * RW_W:3 * RW_W]
    small = rw[:, 3 * RW_W:RW_PAD]

    log_w = -_softplus(-(w0_ref[...] + _dot(jnp.tanh(small), w2_ref[...]))) - 0.5
    a = _sigmoid(a0_ref[...] + _dot(small, a2_ref[...]))
    g = _dot(_sigmoid(small), g2_ref[...])
    if has_vres:
        v = v + (vf_ref[0] - v) * _sigmoid(v0_ref[...] + _dot(small, v2_ref[...]))

    kk = k * kk_ref[...]
    ss = _dot_sel_right(kk * kk, _group_ones(RW_W, RW_HD))
    kk = kk * lax.rsqrt(jnp.maximum(ss, 1e-24))

    r_o[0] = r
    k_o[0] = k * (1.0 + (a - 1.0) * ka_ref[...])
    v_o[0] = v
    lw_o[0] = -jnp.exp(log_w)
    kk_o[0] = kk
    a_o[0] = a
    g_o[0] = g


def _rwkv_prep(proj3, mu, w0, w2p, a0, a2p, g2p, k_k, k_a, vres):
    b, s, _ = proj3.shape
    ts = min(512, s)
    has_vres = vres is not None
    row = lambda t: t.reshape(1, -1)
    vec = lambda n: pl.BlockSpec((1, n), lambda i, j: (0, 0))
    lora = pl.BlockSpec((RW_SMALL, RW_W), lambda i, j: (0, 0))
    seq = pl.BlockSpec((1, ts, RW_W), lambda i, j: (i, j, 0))
    args = [proj3, row(mu), row(w0), w2p, row(a0), a2p, g2p, row(k_k), row(k_a)]
    specs = [pl.BlockSpec((1, ts, RW_PAD), lambda i, j: (i, j, 0)), vec(RW_PAD), vec(RW_W), lora,
             vec(RW_W), lora, lora, vec(RW_W), vec(RW_W)]
    if has_vres:
        v0, v2p, v_first = vres
        args += [row(v0), v2p, v_first]
        specs += [vec(RW_W), lora, seq]
    out = jax.ShapeDtypeStruct((b, s, RW_W), F32)
    return pl.pallas_call(
        functools.partial(_rwkv_prep_kernel, has_vres=has_vres),
        out_shape=[out] * 7,
        grid=(b, s // ts),
        in_specs=specs,
        out_specs=[seq] * 7,
        scratch_shapes=[pltpu.VMEM((1, RW_PAD), F32)],
        compiler_params=_cparams(("parallel", "arbitrary")),
        name="rwkv_prep",
    )(*args)


def _dot3(a, b):
    ah, al = a if isinstance(a, tuple) else _split(a)
    bh, bl = b if isinstance(b, tuple) else _split(b)
    d = lambda x, y: jnp.dot(x, y, preferred_element_type=F32)
    return d(ah, bh) + (d(ah, bl) + d(al, bh))


def _unit_lower_inverses(ns, l):
    size = ns[0].shape[0]
    row, col = _iota2((size, size), 0), _iota2((size, size), 1)
    eye = (row == col).astype(F32)
    same = lambda blk: (row // blk) == (col // blk)
    n1 = [_split(jnp.where(same(8), n, 0.0)) for n in ns]
    n2 = [_dot3(a, a) for a in n1]
    n2s = [_split(a) for a in n2]
    n4 = [_dot3(a, a) for a in n2s]
    n3 = [_dot3(a, b) for a, b in zip(n1, n2s)]
    p1 = [eye + jnp.where(same(8), n, 0.0) + b + c for n, b, c in zip(ns, n2, n3)]
    t = [p + _dot3(p, q) for p, q in zip(p1, n4)]
    blk = 8
    while blk < l:
        level = same(2 * blk) & jnp.logical_not(same(blk))
        tb = [a.astype(BF16) for a in t]
        tn = [_dot(a, jnp.where(level, n, 0.0)) for a, n in zip(tb, ns)]
        t = [a + _dot(b, c) for a, b, c in zip(t, tn, tb)]
        blk *= 2
    return t


def _rwkv_chunk_kernel(r_ref, k_ref, v_ref, lw_ref, kk_ref, a_ref, g_ref, rk_ref, lnw_ref, lnb_ref,
                       o_ref, h_scr):
    l = RW_CHUNK

    @pl.when(pl.program_id(1) == 0)
    def _():
        h_scr[...] = jnp.zeros_like(h_scr)

    pairs = range(RW_W // LANES)
    n_chunks = r_ref.shape[1] // l
    units = [(c, p) for c in range(n_chunks) for p in pairs]
    rws = [slice(c * l, (c + 1) * l) for c, _ in units]
    sls = [slice(p * LANES, (p + 1) * LANES) for _, p in units]
    tri_incl = (_iota2((l, l), 1) <= _iota2((l, l), 0)).astype(BF16)
    m0 = _iota2((1, LANES), 1) < RW_HD
    rr, cc = _iota2((2 * l, 2 * l), 0), _iota2((2 * l, 2 * l), 1)
    same_head = (rr // l) == (cc // l)
    strict = same_head & ((rr % l) > (cc % l))
    incl = same_head & ((rr % l) >= (cc % l))
    eye = _iota2((LANES, LANES), 0) == _iota2((LANES, LANES), 1)
    head_ones = _group_ones(LANES, RW_HD)
    stack = lambda t: jnp.concatenate([jnp.where(m0, t, 0.0), jnp.where(m0, 0.0, t)], axis=0)
    unstack = lambda t: t[:l] + t[l:]

    load = lambda ref: [ref[0, rw, s] for rw, s in zip(rws, sls)]
    r, k, v, lw, kk_n, a_lr = load(r_ref), load(k_ref), load(v_ref), load(lw_ref), load(kk_ref), load(a_ref)
    gc = [_dot_sel_left(tri_incl, x) for x in lw]
    g_last = [g[l - 1:l, :] for g in gc]
    e_neg = [jnp.exp(-g) for g in gc]
    e_end = [jnp.exp(gl - g) for gl, g in zip(g_last, gc)]
    bv = [x * y for x, y in zip(kk_n, a_lr)]
    a_s = [stack(-x * jnp.exp(g - w)) for x, g, w in zip(kk_n, gc, lw)]
    r_s = [stack(x * jnp.exp(g)) for x, g in zip(r, gc)]
    b_s = [stack(x * e) for x, e in zip(bv, e_neg)]
    k_s = [stack(x * e) for x, e in zip(k, e_neg)]
    v_s = [stack(x) for x in v]
    end_s = [jnp.concatenate([stack(x * e), stack(y * e)], axis=0) for x, y, e in zip(bv, k, e_end)]

    prod = [_dot_nt(jnp.concatenate([a, rq], axis=0), jnp.concatenate([b, kq], axis=0))
            for a, rq, b, kq in zip(a_s, r_s, b_s, k_s)]
    t_inv = _unit_lower_inverses([jnp.where(strict, x[:2 * l, :2 * l], 0.0) for x in prod], l)
    x_s = [_dot(jnp.where(strict, pr[:2 * l, 2 * l:], 0.0), vs) for pr, vs in zip(prod, v_s)]
    wu = [_dot(t, jnp.concatenate([a, x], axis=1)) for t, a, x in zip(t_inv, a_s, x_s)]
    big = [jnp.concatenate([w, jnp.concatenate([jnp.zeros_like(vs), vs], axis=1)], axis=0)
           for w, vs in zip(wu, v_s)]
    r_bk = [jnp.concatenate([jnp.where(incl, pr[2 * l:, :2 * l], 0.0), jnp.where(incl, pr[2 * l:, 2 * l:], 0.0)],
                            axis=1) for pr in prod]
    qy = [_dot(x, b) for x, b in zip(r_bk, big)]
    mn = [_dot_tn(e, b) for e, b in zip(end_s, big)]

    q_s = [rq + x[:, :LANES] for rq, x in zip(r_s, qy)]
    m_mat = [x[:, :LANES] + jnp.where(eye, jnp.broadcast_to(jnp.exp(gl), (LANES, LANES)), 0.0)
             for x, gl in zip(mn, g_last)]
    state = [h_scr[p] for p in pairs]
    y = []
    for c in range(n_chunks):
        u0 = c * len(pairs)
        y += [unstack(_dot(q_s[u0 + p], state[p]) + qy[u0 + p][:, LANES:]) for p in pairs]
        state = [_dot3(m_mat[u0 + p], state[p]) + mn[u0 + p][:, LANES:] for p in pairs]
    for p in pairs:
        h_scr[p] = state[p]

    for u, (rw, sl) in enumerate(zip(rws, sls)):
        mean = _dot_sel_right(y[u], head_ones) * (1.0 / RW_HD)
        yc = y[u] - mean
        var = _dot_sel_right(yc * yc, head_ones) * (1.0 / RW_HD)
        yn = yc * lax.rsqrt(var + GN_EPS) * lnw_ref[:, sl] + lnb_ref[:, sl]
        bonus = _dot_sel_right(r[u] * k[u] * rk_ref[:, sl], head_ones) * v[u]
        o_ref[0, rw, sl] = ((yn + bonus) * g_ref[0, rw, sl]).astype(o_ref.dtype)


def _rwkv_chunk(r, k, v, lw, kk, a, g, r_k, ln_w, ln_b):
    b, s, _ = r.shape
    ts = min(RW_TS, s)
    seq = pl.BlockSpec((1, ts, RW_W), lambda i, j: (i, j, 0))
    vec = pl.BlockSpec((1, RW_W), lambda i, j: (0, 0))
    return pl.pallas_call(
        _rwkv_chunk_kernel,
        out_shape=jax.ShapeDtypeStruct((b, s, RW_W), BF16),
        grid=(b, s // ts),
        in_specs=[seq] * 7 + [vec] * 3,
        out_specs=seq,
        scratch_shapes=[pltpu.VMEM((RW_W // LANES, LANES, LANES), F32)],
        compiler_params=_cparams(("parallel", "arbitrary")),
        name="rwkv_chunk",
    )(r, k, v, lw, kk, a, g, r_k.reshape(1, -1), ln_w.reshape(1, -1), ln_b.reshape(1, -1))


def _pair_rmsnorm(t, gain_row, ones):
    ms = _dot_sel_right(t * t, ones) * (1.0 / SB_HD)
    return t * lax.rsqrt(ms + NORM_EPS) * gain_row


def _sb_kernel(q_ref, k_ref, v_ref, qg_ref, kg_ref, o_ref, kn_scr, vb_scr, qs_scr, c_scr, acc_scr):
    tq = q_ref.shape[1]
    tk = SB_TK
    n_tiles = q_ref.shape[2] // LANES
    qb = pl.program_id(2)
    ones = _group_ones(LANES, SB_HD)
    lanes = [slice(t * LANES, (t + 1) * LANES) for t in range(n_tiles)]

    @pl.when(qb == 0)
    def _():
        for sl in lanes:
            kn_scr[:, sl] = _pair_rmsnorm(k_ref[0, :, sl], kg_ref[:, sl], ones).astype(BF16)
        vb_scr[...] = v_ref[0].astype(BF16)

    m0 = _iota2((1, LANES), 1) < SB_HD
    for t, sl in enumerate(lanes):
        q = _pair_rmsnorm(q_ref[0, :, sl], qg_ref[:, sl], ones) * (SB_HD ** -0.5)
        qs_scr[(2 * t) * tq:(2 * t + 1) * tq, :] = jnp.where(m0, q, 0.0).astype(BF16)
        qs_scr[(2 * t + 1) * tq:(2 * t + 2) * tq, :] = jnp.where(m0, 0.0, q).astype(BF16)
    c_scr[...] = jnp.zeros_like(c_scr)
    acc_scr[...] = jnp.zeros_like(acc_scr)

    tri_after = (_iota2((2 * tk, 2 * tk), 0) > _iota2((2 * tk, 2 * tk), 1)).astype(BF16)

    def tile_pair(j_old, masked):
        start = pl.multiple_of(j_old * tk, tk)
        if masked:
            q_pos = qb * tq + _iota2((tq, 2 * tk), 0)
            before = (start + _iota2((tq, 2 * tk), 1)) < q_pos
        for t, sl in enumerate(lanes):
            kk = kn_scr[pl.ds(start, 2 * tk), sl]
            vv = vb_scr[pl.ds(start, 2 * tk), sl]
            for h in range(2):
                rows = slice((2 * t + h) * tq, (2 * t + h + 1) * tq)
                z = lax.dot_general(qs_scr[rows, :], kk, (((1,), (1,)), ((), ())), preferred_element_type=F32)
                drop = jnp.maximum(z, 0.0) + jnp.log(1.0 + jnp.exp2(jnp.abs(z) * (-LOG2E)))
                if masked:
                    drop = jnp.where(before, drop, 0.0)
                after = jnp.dot(drop.astype(BF16), tri_after, preferred_element_type=F32)
                c = c_scr[rows, :]
                wgt = jnp.exp(z - (drop + after + c))
                if masked:
                    wgt = jnp.where(before, wgt, 0.0)
                c_scr[rows, :] = c + jnp.sum(drop, axis=1, keepdims=True)
                acc_scr[rows, :] += jnp.dot(wgt.astype(BF16), vv, preferred_element_type=F32)

    tiles_per_block = tq // tk
    for i in reversed(range(tiles_per_block // 2)):
        tile_pair(qb * tiles_per_block + 2 * i, True)

    def body(it, carry):
        tile_pair(qb * tiles_per_block - 2 - 2 * it, False)
        return carry

    lax.fori_loop(0, qb * (tiles_per_block // 2), body, 0)
    for t, sl in enumerate(lanes):
        o_ref[0, :, sl] = jnp.where(m0, acc_scr[(2 * t) * tq:(2 * t + 1) * tq, :],
                                    acc_scr[(2 * t + 1) * tq:(2 * t + 2) * tq, :]).astype(o_ref.dtype)


def _sb_attention(proj3, q_gain, k_gain, col0):
    b, s, _ = proj3.shape
    tq = min(SB_TQ, s)
    assert tq % (2 * SB_TK) == 0 and s % tq == 0
    w = SB_LANE_TILES * LANES
    nblk = SB_W // w
    c0 = col0 // SB_LANE_TILES
    gains = lambda g: jnp.tile(g, w // SB_HD).reshape(1, w)
    rows = SB_LANE_TILES * 2 * tq
    return pl.pallas_call(
        _sb_kernel,
        out_shape=jax.ShapeDtypeStruct((b, s, SB_W), BF16),
        grid=(b, nblk, s // tq),
        in_specs=[
            pl.BlockSpec((1, tq, w), lambda i, p, j: (i, j, c0 + p)),
            pl.BlockSpec((1, s, w), lambda i, p, j: (i, 0, c0 + nblk + p), pipeline_mode=pl.Buffered(1)),
            pl.BlockSpec((1, s, w), lambda i, p, j: (i, 0, c0 + 2 * nblk + p), pipeline_mode=pl.Buffered(1)),
            pl.BlockSpec((1, w), lambda i, p, j: (0, 0)),
            pl.BlockSpec((1, w), lambda i, p, j: (0, 0)),
        ],
        out_specs=pl.BlockSpec((1, tq, w), lambda i, p, j: (i, j, p)),
        scratch_shapes=[pltpu.VMEM((s, w), BF16), pltpu.VMEM((s, w), BF16),
                        pltpu.VMEM((rows, LANES), BF16), pltpu.VMEM((rows, 1), F32),
                        pltpu.VMEM((rows, LANES), F32)],
        compiler_params=_cparams(("parallel", "parallel", "arbitrary")),
        name="stick_breaking",
    )(proj3, proj3, proj3, gains(q_gain), gains(k_gain))


def _s5_tables(a_re, a_im, b_re, b_im, c_re, c_im, log_dt, l):
    lam = lax.complex(jnp.minimum(a_re.astype(F32), -1e-4), a_im.astype(F32))
    lam_dt = lam * jnp.exp(log_dt.astype(F32))[:, None]
    lam_bar = jnp.exp(lam_dt)
    b_bar = ((lam_bar - 1.0) / lam)[..., None] * lax.complex(b_re.astype(F32), b_im.astype(F32))
    c = lax.complex(c_re.astype(F32), c_im.astype(F32))
    steps = jnp.arange(l + 1, dtype=F32)
    powers = jnp.exp(lam_dt[:, None, :] * steps[None, :, None])
    g, p = lam.shape
    kern = jnp.einsum('gcp,gjp,gpd->gjcd', c, powers[:, :l], b_bar).real
    lag = jnp.arange(l)[None, :] - jnp.arange(l)[:, None]
    toep = jnp.where((lag >= 0)[None, :, :, None, None], kern[:, jnp.clip(lag, 0, l - 1)], 0.0)
    toep = toep.transpose(0, 1, 4, 2, 3).reshape(g, l * S5_GROUP, l * S5_GROUP)
    b_end = powers[:, :l][:, ::-1, :, None] * b_bar[:, None, :, :]
    b_end = b_end.transpose(0, 1, 3, 2).reshape(g, l * S5_GROUP, p)
    b_end = jnp.concatenate([b_end.real, b_end.imag], axis=-1)
    c_pow = c[:, None, :, :] * powers[:, 1:, None, :]
    c_pow = jnp.concatenate([c_pow.real, -c_pow.imag], axis=-1)
    c_pow = c_pow.transpose(0, 3, 1, 2).reshape(g, 2 * p, l * S5_GROUP)
    lam_l = powers[:, l]
    mul_same = jnp.concatenate([lam_l.real, lam_l.real], axis=-1).reshape(g, 1, 2 * p)
    mul_swap = jnp.concatenate([-lam_l.imag, lam_l.imag], axis=-1).reshape(g, 1, 2 * p)
    return toep.astype(BF16), b_end.astype(BF16), c_pow.astype(BF16), mul_same, mul_swap


def _s5_kernel(u_ref, toep_ref, bend_ref, cpow_ref, same_ref, swap_ref, y_ref, e_scr, x_scr, *, bsz):
    u = u_ref[0]
    e_scr[...] = jnp.dot(u, bend_ref[0], preferred_element_type=F32)
    n_chunks = u.shape[0] // bsz
    same, swap = same_ref[0], swap_ref[0]

    def body(c, x):
        rows = pl.ds(pl.multiple_of(c * bsz, bsz), bsz)
        x_scr[rows, :] = x
        return x * same + pltpu.roll(x, S5_P, axis=1) * swap + e_scr[rows, :]

    lax.fori_loop(0, n_chunks, body, jnp.zeros((bsz, 2 * S5_P), F32))
    y_ref[0] = (jnp.dot(u, toep_ref[0], preferred_element_type=F32)
                + jnp.dot(x_scr[...].astype(BF16), cpow_ref[0], preferred_element_type=F32))


def _s5_ssm(u, tables):
    b, s, _ = u.shape
    l = min(S5_CHUNK, s)
    nc = s // l
    toep, b_end, c_pow, mul_same, mul_swap = tables
    width = l * S5_GROUP
    ug = u.astype(BF16).reshape(b, nc, l, S5_G, S5_GROUP).transpose(3, 1, 0, 2, 4).reshape(S5_G, nc * b, width)
    grp = lambda shape: pl.BlockSpec((1,) + shape, lambda g: (g, 0, 0))
    y = pl.pallas_call(
        functools.partial(_s5_kernel, bsz=b),
        out_shape=jax.ShapeDtypeStruct((S5_G, nc * b, width), F32),
        grid=(S5_G,),
        in_specs=[grp((nc * b, width)), grp((width, width)), grp((width, 2 * S5_P)),
                  grp((2 * S5_P, width)), grp((1, 2 * S5_P)), grp((1, 2 * S5_P))],
        out_specs=grp((nc * b, width)),
        scratch_shapes=[pltpu.VMEM((nc * b, 2 * S5_P), F32), pltpu.VMEM((nc * b, 2 * S5_P), F32)],
        compiler_params=_cparams(("parallel",)),
        name="s5_ssm",
    )(ug, toep, b_end, c_pow, mul_same, mul_swap)
    return y.reshape(S5_G, nc, b, l, S5_GROUP).transpose(2, 1, 3, 0, 4).reshape(b, s, S5_W)


def _s5_post_kernel(y_ref, u_ref, d_ref, w_ref, b_ref, o_ref):
    y = y_ref[...] + d_ref[...] * u_ref[...]
    y = 0.5 * y * (1.0 + jnp.tanh(0.7978845608028654 * (y + 0.044715 * (y * y * y))))
    o_ref[...] = (y * _sigmoid(_dot(y, w_ref[...]) + b_ref[...])).astype(o_ref.dtype)


def _s5_post(y2, proj2, d, w_glu, b_glu):
    n = y2.shape[0]
    tm = min(2048, n)
    vec = pl.BlockSpec((1, S5_W), lambda i: (0, 0))
    return pl.pallas_call(
        _s5_post_kernel,
        out_shape=jax.ShapeDtypeStruct((n, S5_W), BF16),
        grid=(n // tm,),
        in_specs=[pl.BlockSpec((tm, S5_W), lambda i: (i, 0)),
                  pl.BlockSpec((tm, S5_W), lambda i: (i, ODD_IN // S5_W - 1)),
                  vec, pl.BlockSpec((S5_W, S5_W), lambda i: (0, 0)), vec],
        out_specs=pl.BlockSpec((tm, S5_W), lambda i: (i, 0)),
        compiler_params=_cparams(("parallel",)),
        name="s5_glu",
    )(y2, proj2, d.reshape(1, -1), w_glu.astype(BF16), b_glu.reshape(1, -1))


def _hgrn2_kernel(q_ref, f_ref, i_ref, gate_ref, loglb_ref, log1mlb_ref, onemlb_ref, gain_ref,
                  o_ref, st_scr):
    ts = q_ref.shape[1]
    lb = HG_BLOCK
    heads = range(HG_HEADS)
    sls = [slice(h * HG_DK, (h + 1) * HG_DK) for h in heads]

    @pl.when(pl.program_id(1) == 0)
    def _():
        st_scr[...] = jnp.zeros_like(st_scr)

    rr, cc = _iota2((ts, ts), 0), _iota2((ts, ts), 1)
    same = (rr // lb) == (cc // lb)
    causal = same & (cc <= rr)
    tri, blk = causal.astype(BF16), same.astype(BF16)
    sel = jnp.concatenate([jnp.concatenate([tri, tri], axis=1), jnp.concatenate([blk, blk], axis=1)], axis=0)

    fl = f_ref[0]
    hi_arg = loglb_ref[...]
    lo_arg = log1mlb_ref[...] - _softplus(-fl)
    log_f = jnp.maximum(hi_arg, lo_arg) + jnp.log(1.0 + jnp.exp(-jnp.abs(hi_arg - lo_arg)))
    kk = onemlb_ref[...] * _sigmoid(-fl)
    hi, lo = _split(log_f)
    gg = jnp.dot(sel, jnp.concatenate([hi, lo], axis=0), preferred_element_type=F32)
    g_cum, g_tot = gg[:ts], gg[ts:]
    q_in = q_ref[0] * jnp.exp(g_cum)
    k_in = kk * jnp.exp(-g_cum)
    k_end = kk * jnp.exp(g_tot - g_cum)
    decay = jnp.exp(g_tot)
    val = i_ref[0]
    att = [jnp.where(causal, _dot_nt(q_in[:, s], k_in[:, s]), 0.0) for s in sls]
    intra = [_dot(a, val[:, s]) for a, s in zip(att, sls)]
    n_sub = ts // lb
    rows = [slice(c * lb, (c + 1) * lb) for c in range(n_sub)]
    push = [[_dot_tn(val[r, s], k_end[r, s]) for s in sls] for r in rows]

    st = [st_scr[h] for h in heads]
    inter = []
    for c, r in enumerate(rows):
        inter.append([_dot_nt(q_in[r, s], x) for s, x in zip(sls, st)])
        st = [x * decay[c * lb:c * lb + 1, s] + p for x, s, p in zip(st, sls, push[c])]
    for h in heads:
        st_scr[h] = st[h]

    gate = gate_ref[0]
    for h, s in zip(heads, sls):
        o = intra[h] + jnp.concatenate([inter[c][h] for c in range(n_sub)], axis=0)
        o = o * lax.rsqrt(jnp.mean(o * o, axis=-1, keepdims=True) + NORM_EPS) * gain_ref[:, s]
        o_ref[0, :, s] = (o * (gate[:, s] * _sigmoid(gate[:, s]))).astype(o_ref.dtype)


def _hgrn2(proj3, lb, gain):
    b, s, _ = proj3.shape
    ts = min(HG_TS, s)
    w = HG_KW
    lb = lb.astype(F32).reshape(1, w)
    col = lambda c: pl.BlockSpec((1, ts, w), lambda i, j: (i, j, c))
    vec = pl.BlockSpec((1, w), lambda i, j: (0, 0))
    return pl.pallas_call(
        _hgrn2_kernel,
        out_shape=jax.ShapeDtypeStruct((b, s, w), BF16),
        grid=(b, s // ts),
        in_specs=[col(0), col(1), col(2), col(3)] + [vec] * 4,
        out_specs=col(0),
        scratch_shapes=[pltpu.VMEM((HG_HEADS, HG_DK, HG_DK), F32)],
        compiler_params=_cparams(("parallel", "arbitrary")),
        name="hgrn2",
    )(proj3, proj3, proj3, proj3, jnp.log(lb), jnp.log1p(-lb), 1.0 - lb, gain.reshape(1, w))


def _pad_rows(w, rows, start):
    return jnp.zeros((rows, w.shape[1]), BF16).at[start:start + w.shape[0]].set(w.astype(BF16))


def _even_mixer(x2, bsz, seq, norm, w_in, w_out, mu, w0, w2, a0, a2, g2, k_k, k_a, r_k, ln_w, ln_b,
                q_gain, k_gain, v_first, v_mix):
    d = x2.shape[1]
    rw_cols = 3 * RW_W + W_LORA + A_LORA + G_LORA
    pad = RW_PAD - rw_cols
    extra = jnp.zeros((d, pad), w_in.dtype)
    mu_p = jnp.concatenate([mu, jnp.zeros((pad,), mu.dtype)])
    vres = None
    if v_mix is not None:
        v0, v1, v2 = v_mix
        extra = extra.at[:, :V_LORA].set(v1)
        vres = (v0, _pad_rows(v2, RW_SMALL, W_LORA + A_LORA + G_LORA), v_first)
    gap = jnp.zeros((d, SB_COL0 - RW_PAD), w_in.dtype)
    w_pad = jnp.concatenate([w_in[:, :rw_cols], extra, gap, w_in[:, rw_cols:]], axis=1)
    proj3 = _proj(x2, norm, w_pad).reshape(bsz, seq, EVEN_PAD)
    r, k, v, lw, kk, a, g = _rwkv_prep(
        proj3, mu_p, w0, _pad_rows(w2, RW_SMALL, 0), a0, _pad_rows(a2, RW_SMALL, W_LORA),
        _pad_rows(g2, RW_SMALL, W_LORA + A_LORA), k_k, k_a, vres)
    y_rw = _rwkv_chunk(r, k, v, lw, kk, a, g, r_k, ln_w, ln_b)
    y_sb = _sb_attention(proj3, q_gain, k_gain, SB_COL0 // LANES)
    n = bsz * seq
    out = _outproj(x2, y_rw.reshape(n, RW_W), y_sb.reshape(n, SB_W), w_out)
    return out, v


def _odd_mixer(x2, bsz, seq, norm, w_in, w_out, s5_params, d_skip, w_glu, b_glu, lb, hg_gain):
    n = bsz * seq
    w_perm = jnp.concatenate([w_in[:, S5_W:], w_in[:, :S5_W]], axis=1)
    proj2 = _proj(x2, norm, w_perm)
    proj3 = proj2.reshape(bsz, seq, ODD_IN)
    tables = _s5_tables(*s5_params, min(S5_CHUNK, seq))
    ssm = _s5_ssm(proj3[..., ODD_IN - S5_W:], tables)
    y_s5 = _s5_post(ssm.reshape(n, S5_W), proj2, d_skip, w_glu, b_glu)
    y_hg = _hgrn2(proj3, lb, hg_gain)
    return _outproj(x2, y_s5, y_hg.reshape(n, HG_VW), w_out)


def kernel(x, ffn1_norm, ffn1_w13, ffn1_w2, mix_norm, ffn2_norm, ffn2_w13, ffn2_w2, ev_w_in, ev_w_out, rw_mu, rw_w0, rw_w2, rw_a0, rw_a2, rw_g2, rw_k_k, rw_k_a, rw_r_k, rw_ln_w, rw_ln_b, rw_v0, rw_v1, rw_v2, sb_q_gain, sb_k_gain, od_w_in, od_w_out, s5_a_re, s5_a_im, s5_b_re, s5_b_im, s5_c_re, s5_c_im, s5_d, s5_log_dt, s5_w_glu, s5_b_glu, hg_lb, hg_gain):
    bsz, seq, d = x.shape
    depth = ffn1_norm.shape[0]
    lb_all = jnp.cumsum(jax.nn.softmax(hg_lb.astype(F32), axis=0), axis=0)
    lb_all = lb_all - lb_all[0]
    x2 = x.reshape(bsz * seq, d)
    v_first = None
    for layer in range(depth):
        x2 = _ffn(x2, ffn1_norm[layer], ffn1_w13[layer], ffn1_w2[layer])
        if layer % 2 == 0:
            e = layer // 2
            v_mix = None if e == 0 else (rw_v0[e - 1], rw_v1[e - 1], rw_v2[e - 1])
            x2, v = _even_mixer(x2, bsz, seq, mix_norm[layer], ev_w_in[e], ev_w_out[e], rw_mu[e], rw_w0[e],
                                rw_w2[e], rw_a0[e], rw_a2[e], rw_g2[e], rw_k_k[e], rw_k_a[e], rw_r_k[e],
                                rw_ln_w[e], rw_ln_b[e], sb_q_gain[e], sb_k_gain[e], v_first, v_mix)
            if e == 0:
                v_first = v
        else:
            o = layer // 2
            s5_params = (s5_a_re[o], s5_a_im[o], s5_b_re[o], s5_b_im[o], s5_c_re[o], s5_c_im[o], s5_log_dt[o])
            x2 = _odd_mixer(x2, bsz, seq, mix_norm[layer], od_w_in[o], od_w_out[o], s5_params, s5_d[o],
                            s5_w_glu[o], s5_b_glu[o], lb_all[o], hg_gain[o])
        x2 = _ffn(x2, ffn2_norm[layer], ffn2_w13[layer], ffn2_w2[layer])
    return x2.reshape(bsz, seq, d)
```

```python
import functools

import jax
import jax.numpy as jnp
from jax import lax
from jax.experimental import pallas as pl
from jax.experimental.pallas import tpu as pltpu

F32 = jnp.float32
BF16 = jnp.bfloat16

LOG2E = 1.4426950408889634
NORM_EPS = 1e-6
GN_EPS = 64e-5
LANES = 128
RW_HEADS, RW_HD, RW_W = 8, 64, 512
W_LORA, A_LORA, V_LORA, G_LORA = 32, 32, 32, 96
RW_SMALL = 256
RW_PAD = 3 * RW_W + RW_SMALL
SB_W, SB_HD = 512, 64
SB_COL0 = 2048
EVEN_PAD = SB_COL0 + 3 * SB_W
S5_G, S5_GROUP, S5_W, S5_P = 16, 16, 256, 64
HG_HEADS, HG_DK, HG_KW, HG_VW, HG_BLOCK = 6, 128, 768, 768, 16
ODD_IN = S5_W + 2 * HG_KW + 2 * HG_VW
VMEM_LIMIT = 48 * 1024 * 1024

FFN_TM = 512
FFN_CHUNK = 256
RW_CHUNK = 64
RW_TS = 256
S5_CHUNK = 64
SB_TQ = 512
SB_TK = 128
SB_LANE_TILES = 4
HG_TS = 128


def _cparams(sem):
    return pltpu.CompilerParams(dimension_semantics=sem, vmem_limit_bytes=VMEM_LIMIT)


def _dot(a, b):
    return jnp.dot(a.astype(BF16), b.astype(BF16), preferred_element_type=F32)


def _dot_nt(a, b):
    return lax.dot_general(a.astype(BF16), b.astype(BF16), (((1,), (1,)), ((), ())),
                           preferred_element_type=F32)


def _dot_tn(a, b):
    return lax.dot_general(a.astype(BF16), b.astype(BF16), (((0,), (0,)), ((), ())),
                           preferred_element_type=F32)


def _dot_hi(a, b):
    return jnp.dot(a, b, preferred_element_type=F32, precision=lax.Precision.HIGHEST)


def _split(x):
    hi = x.astype(BF16)
    lo = (x - hi.astype(F32)).astype(BF16)
    return hi, lo


def _dot_sel_right(x, e):
    hi, lo = _split(x)
    return (jnp.dot(hi, e, preferred_element_type=F32) + jnp.dot(lo, e, preferred_element_type=F32))


def _dot_sel_left(e, x):
    hi, lo = _split(x)
    return (jnp.dot(e, hi, preferred_element_type=F32) + jnp.dot(e, lo, preferred_element_type=F32))


def _softplus(x):
    return jnp.maximum(x, 0.0) + jnp.log(1.0 + jnp.exp(-jnp.abs(x)))


def _sigmoid(x):
    return jax.nn.sigmoid(x)


def _iota2(shape, dim):
    return lax.broadcasted_iota(jnp.int32, shape, dim)


def _group_ones(n, group):
    return (_iota2((n, n), 0) // group == _iota2((n, n), 1) // group).astype(BF16)


def _rms_rows(x, gain):
    return x * lax.rsqrt(jnp.mean(x * x, axis=-1, keepdims=True) + NORM_EPS) * gain


def _ffn_kernel(x_ref, g_ref, w13_ref, w2_ref, o_ref, h_scr, act_scr):
    dff = w2_ref.shape[0]
    h_scr[...] = _rms_rows(x_ref[...], g_ref[...]).astype(BF16)
    for j in range(dff // FFN_CHUNK):
        lo = j * FFN_CHUNK
        gate = jnp.dot(h_scr[...], w13_ref[:, lo:lo + FFN_CHUNK], preferred_element_type=F32)
        up = jnp.dot(h_scr[...], w13_ref[:, dff + lo:dff + lo + FFN_CHUNK], preferred_element_type=F32)
        act_scr[:, lo:lo + FFN_CHUNK] = (gate * _sigmoid(gate) * up).astype(BF16)
    o_ref[...] = x_ref[...] + 0.5 * jnp.dot(act_scr[...], w2_ref[...], preferred_element_type=F32)


def _ffn(x2, gain, w13, w2):
    n, d = x2.shape
    dff = w2.shape[0]
    tm = min(FFN_TM, n)
    resident = lambda shape: pl.BlockSpec(shape, lambda i: (0, 0), pipeline_mode=pl.Buffered(1))
    return pl.pallas_call(
        _ffn_kernel,
        out_shape=jax.ShapeDtypeStruct((n, d), F32),
        grid=(n // tm,),
        in_specs=[
            pl.BlockSpec((tm, d), lambda i: (i, 0)),
            resident((1, d)),
            resident((d, 2 * dff)),
            resident((dff, d)),
        ],
        out_specs=pl.BlockSpec((tm, d), lambda i: (i, 0)),
        scratch_shapes=[pltpu.VMEM((tm, d), BF16), pltpu.VMEM((tm, dff), BF16)],
        compiler_params=_cparams(("parallel",)),
        name="ffn",
    )(x2, gain.reshape(1, d), w13.astype(BF16), w2.astype(BF16))


def _proj_kernel(x_ref, g_ref, w_ref, o_ref):
    h = _rms_rows(x_ref[...], g_ref[...]).astype(BF16)
    o_ref[...] = jnp.dot(h, w_ref[...], preferred_element_type=F32)


def _proj(x2, gain, w):
    n, d = x2.shape
    c = w.shape[1]
    tm = min(FFN_TM, n)
    return pl.pallas_call(
        _proj_kernel,
        out_shape=jax.ShapeDtypeStruct((n, c), F32),
        grid=(n // tm,),
        in_specs=[
            pl.BlockSpec((tm, d), lambda i: (i, 0)),
            pl.BlockSpec((1, d), lambda i: (0, 0), pipeline_mode=pl.Buffered(1)),
            pl.BlockSpec((d, c), lambda i: (0, 0), pipeline_mode=pl.Buffered(1)),
        ],
        out_specs=pl.BlockSpec((tm, c), lambda i: (i, 0)),
        compiler_params=_cparams(("parallel",)),
        name="mixer_in_proj",
    )(x2, gain.reshape(1, d), w.astype(BF16))


def _outproj_kernel(x_ref, a1_ref, a2_ref, w1_ref, w2_ref, o_ref):
    o_ref[...] = (x_ref[...]
                  + jnp.dot(a1_ref[...], w1_ref[...], preferred_element_type=F32)
                  + jnp.dot(a2_ref[...], w2_ref[...], preferred_element_type=F32))


def _outproj(x2, a1, a2, w_out):
    n, d = x2.shape
    k1, k2 = a1.shape[1], a2.shape[1]
    tm = min(1024, n)
    w = w_out.astype(BF16)
    return pl.pallas_call(
        _outproj_kernel,
        out_shape=jax.ShapeDtypeStruct((n, d), F32),
        grid=(n // tm,),
        in_specs=[
            pl.BlockSpec((tm, d), lambda i: (i, 0)),
            pl.BlockSpec((tm, k1), lambda i: (i, 0)),
            pl.BlockSpec((tm, k2), lambda i: (i, 0)),
            pl.BlockSpec((k1, d), lambda i: (0, 0)),
            pl.BlockSpec((k2, d), lambda i: (0, 0)),
        ],
        out_specs=pl.BlockSpec((tm, d), lambda i: (i, 0)),
        compiler_params=_cparams(("parallel",)),
        name="mixer_out_proj",
    )(x2, a1, a2, w[:k1], w[k1:])


def _rwkv_prep_kernel(*refs, has_vres):
    if has_vres:
        (p_ref, mu_ref, w0_ref, w2_ref, a0_ref, a2_ref, g2_ref, kk_ref, ka_ref, v0_ref, v2_ref, vf_ref,
         r_o, k_o, v_o, lw_o, kk_o, a_o, g_o, carry) = refs
    else:
        (p_ref, mu_ref, w0_ref, w2_ref, a0_ref, a2_ref, g2_ref, kk_ref, ka_ref,
         r_o, k_o, v_o, lw_o, kk_o, a_o, g_o, carry) = refs
    ts = p_ref.shape[1]

    @pl.when(pl.program_id(1) == 0)
    def _():
        carry[...] = jnp.zeros_like(carry)

    x = p_ref[0]
    prev = pltpu.roll(x, 1, axis=0)
    prev = jnp.where(_iota2((ts, 1), 0) == 0, carry[...], prev)
    carry[...] = x[ts - 1:ts, :]
    rw = x + mu_ref[...] * (prev - x)

    r = rw[:, 0:RW_W]
    k = rw[:, RW_W:2 * RW_W]
    v = rw[:, 2 * RW_W:3 * RW_W]
    small = rw[:, 3 * RW_W:RW_PAD]

    log_w = -_softplus(-(w0_ref[...] + _dot(jnp.tanh(small), w2_ref[...]))) - 0.5
    a = _sigmoid(a0_ref[...] + _dot(small, a2_ref[...]))
    g = _dot(_sigmoid(small), g2_ref[...])
    if has_vres:
        v = v + (vf_ref[0] - v) * _sigmoid(v0_ref[...] + _dot(small, v2_ref[...]))

    kk = k * kk_ref[...]
    ss = _dot_sel_right(kk * kk, _group_ones(RW_W, RW_HD))
    kk = kk * lax.rsqrt(jnp.maximum(ss, 1e-24))

    r_o[0] = r
    k_o[0] = k * (1.0 + (a - 1.0) * ka_ref[...])
    v_o[0] = v
    lw_o[0] = -jnp.exp(log_w)
    kk_o[0] = kk
    a_o[0] = a
    g_o[0] = g


def _rwkv_prep(proj3, mu, w0, w2p, a0, a2p, g2p, k_k, k_a, vres):
    b, s, _ = proj3.shape
    ts = min(512, s)
    has_vres = vres is not None
    row = lambda t: t.reshape(1, -1)
    vec = lambda n: pl.BlockSpec((1, n), lambda i, j: (0, 0))
    lora = pl.BlockSpec((RW_SMALL, RW_W), lambda i, j: (0, 0))
    seq = pl.BlockSpec((1, ts, RW_W), lambda i, j: (i, j, 0))
    args = [proj3, row(mu), row(w0), w2p, row(a0), a2p, g2p, row(k_k), row(k_a)]
    specs = [pl.BlockSpec((1, ts, RW_PAD), lambda i, j: (i, j, 0)), vec(RW_PAD), vec(RW_W), lora,
             vec(RW_W), lora, lora, vec(RW_W), vec(RW_W)]
    if has_vres:
        v0, v2p, v_first = vres
        args += [row(v0), v2p, v_first]
        specs += [vec(RW_W), lora, seq]
    out = jax.ShapeDtypeStruct((b, s, RW_W), F32)
    return pl.pallas_call(
        functools.partial(_rwkv_prep_kernel, has_vres=has_vres),
        out_shape=[out] * 7,
        grid=(b, s // ts),
        in_specs=specs,
        out_specs=[seq] * 7,
        scratch_shapes=[pltpu.VMEM((1, RW_PAD), F32)],
        compiler_params=_cparams(("parallel", "arbitrary")),
        name="rwkv_prep",
    )(*args)


def _dot3(a, b):
    ah, al = a if isinstance(a, tuple) else _split(a)
    bh, bl = b if isinstance(b, tuple) else _split(b)
    d = lambda x, y: jnp.dot(x, y, preferred_element_type=F32)
    return d(ah, bh) + (d(ah, bl) + d(al, bh))


def _unit_lower_inverses(ns, l):
    size = ns[0].shape[0]
    row, col = _iota2((size, size), 0), _iota2((size, size), 1)
    eye = (row == col).astype(F32)
    same = lambda blk: (row // blk) == (col // blk)
    n1 = [_split(jnp.where(same(8), n, 0.0)) for n in ns]
    n2 = [_dot3(a, a) for a in n1]
    n2s = [_split(a) for a in n2]
    n4 = [_dot3(a, a) for a in n2s]
    n3 = [_dot3(a, b) for a, b in zip(n1, n2s)]
    p1 = [eye + jnp.where(same(8), n, 0.0) + b + c for n, b, c in zip(ns, n2, n3)]
    t = [p + _dot3(p, q) for p, q in zip(p1, n4)]
    blk = 8
    while blk < l:
        level = same(2 * blk) & jnp.logical_not(same(blk))
        tb = [a.astype(BF16) for a in t]
        tn = [_dot(a, jnp.where(level, n, 0.0)) for a, n in zip(tb, ns)]
        t = [a + _dot(b, c) for a, b, c in zip(t, tn, tb)]
        blk *= 2
    return t


def _rwkv_chunk_kernel(r_ref, k_ref, v_ref, lw_ref, kk_ref, a_ref, g_ref, rk_ref, lnw_ref, lnb_ref,
                       o_ref, h_scr):
    l = RW_CHUNK

    @pl.when(pl.program_id(1) == 0)
    def _():
        h_scr[...] = jnp.zeros_like(h_scr)

    pairs = range(RW_W // LANES)
    n_chunks = r_ref.shape[1] // l
    units = [(c, p) for c in range(n_chunks) for p in pairs]
    rws = [slice(c * l, (c + 1) * l) for c, _ in units]
    sls = [slice(p * LANES, (p + 1) * LANES) for _, p in units]
    tri_incl = (_iota2((l, l), 1) <= _iota2((l, l), 0)).astype(BF16)
    m0 = _iota2((1, LANES), 1) < RW_HD
    rr, cc = _iota2((2 * l, 2 * l), 0), _iota2((2 * l, 2 * l), 1)
    same_head = (rr // l) == (cc // l)
    strict = same_head & ((rr % l) > (cc % l))
    incl = same_head & ((rr % l) >= (cc % l))
    eye = _iota2((LANES, LANES), 0) == _iota2((LANES, LANES), 1)
    head_ones = _group_ones(LANES, RW_HD)
    stack = lambda t: jnp.concatenate([jnp.where(m0, t, 0.0), jnp.where(m0, 0.0, t)], axis=0)
    unstack = lambda t: t[:l] + t[l:]

    load = lambda ref: [ref[0, rw, s] for rw, s in zip(rws, sls)]
    r, k, v, lw, kk_n, a_lr = load(r_ref), load(k_ref), load(v_ref), load(lw_ref), load(kk_ref), load(a_ref)
    gc = [_dot_sel_left(tri_incl, x) for x in lw]
    g_last = [g[l - 1:l, :] for g in gc]
    e_neg = [jnp.exp(-g) for g in gc]
    e_end = [jnp.exp(gl - g) for gl, g in zip(g_last, gc)]
    bv = [x * y for x, y in zip(kk_n, a_lr)]
    a_s = [stack(-x * jnp.exp(g - w)) for x, g, w in zip(kk_n, gc, lw)]
    r_s = [stack(x * jnp.exp(g)) for x, g in zip(r, gc)]
    b_s = [stack(x * e) for x, e in zip(bv, e_neg)]
    k_s = [stack(x * e) for x, e in zip(k, e_neg)]
    v_s = [stack(x) for x in v]
    end_s = [jnp.concatenate([stack(x * e), stack(y * e)], axis=0) for x, y, e in zip(bv, k, e_end)]

    prod = [_dot_nt(jnp.concatenate([a, rq], axis=0), jnp.concatenate([b, kq], axis=0))
            for a, rq, b, kq in zip(a_s, r_s, b_s, k_s)]
    t_inv = _unit_lower_inverses([jnp.where(strict, x[:2 * l, :2 * l], 0.0) for x in prod], l)
    x_s = [_dot(jnp.where(strict, pr[:2 * l, 2 * l:], 0.0), vs) for pr, vs in zip(prod, v_s)]
    wu = [_dot(t, jnp.concatenate([a, x], axis=1)) for t, a, x in zip(t_inv, a_s, x_s)]
    big = [jnp.concatenate([w, jnp.concatenate([jnp.zeros_like(vs), vs], axis=1)], axis=0)
           for w, vs in zip(wu, v_s)]
    r_bk = [jnp.concatenate([jnp.where(incl, pr[2 * l:, :2 * l], 0.0), jnp.where(incl, pr[2 * l:, 2 * l:], 0.0)],
                            axis=1) for pr in prod]
    qy = [_dot(x, b) for x, b in zip(r_bk, big)]
    mn = [_dot_tn(e, b) for e, b in zip(end_s, big)]

    q_s = [rq + x[:, :LANES] for rq, x in zip(r_s, qy)]
    m_mat = [x[:, :LANES] + jnp.where(eye, jnp.broadcast_to(jnp.exp(gl), (LANES, LANES)), 0.0)
             for x, gl in zip(mn, g_last)]
    state = [h_scr[p] for p in pairs]
    y = []
    for c in range(n_chunks):
        u0 = c * len(pairs)
        y += [unstack(_dot(q_s[u0 + p], state[p]) + qy[u0 + p][:, LANES:]) for p in pairs]
        state = [_dot3(m_mat[u0 + p], state[p]) + mn[u0 + p][:, LANES:] for p in pairs]
    for p in pairs:
        h_scr[p] = state[p]

    for u, (rw, sl) in enumerate(zip(rws, sls)):
        mean = _dot_sel_right(y[u], head_ones) * (1.0 / RW_HD)
        yc = y[u] - mean
        var = _dot_sel_right(yc * yc, head_ones) * (1.0 / RW_HD)
        yn = yc * lax.rsqrt(var + GN_EPS) * lnw_ref[:, sl] + lnb_ref[:, sl]
        bonus = _dot_sel_right(r[u] * k[u] * rk_ref[:, sl], head_ones) * v[u]
        o_ref[0, rw, sl] = ((yn + bonus) * g_ref[0, rw, sl]).astype(o_ref.dtype)


def _rwkv_chunk(r, k, v, lw, kk, a, g, r_k, ln_w, ln_b):
    b, s, _ = r.shape
    ts = min(RW_TS, s)
    seq = pl.BlockSpec((1, ts, RW_W), lambda i, j: (i, j, 0))
    vec = pl.BlockSpec((1, RW_W), lambda i, j: (0, 0))
    return pl.pallas_call(
        _rwkv_chunk_kernel,
        out_shape=jax.ShapeDtypeStruct((b, s, RW_W), BF16),
        grid=(b, s // ts),
        in_specs=[seq] * 7 + [vec] * 3,
        out_specs=seq,
        scratch_shapes=[pltpu.VMEM((RW_W // LANES, LANES, LANES), F32)],
        compiler_params=_cparams(("parallel", "arbitrary")),
        name="rwkv_chunk",
    )(r, k, v, lw, kk, a, g, r_k.reshape(1, -1), ln_w.reshape(1, -1), ln_b.reshape(1, -1))


def _pair_rmsnorm(t, gain_row, ones):
    ms = _dot_sel_right(t * t, ones) * (1.0 / SB_HD)
    return t * lax.rsqrt(ms + NORM_EPS) * gain_row


def _sb_kernel(q_ref, k_ref, v_ref, qg_ref, kg_ref, o_ref, kn_scr, vb_scr, qs_scr, c_scr, acc_scr):
    tq = q_ref.shape[1]
    tk = SB_TK
    n_tiles = q_ref.shape[2] // LANES
    qb = pl.program_id(2)
    ones = _group_ones(LANES, SB_HD)
    lanes = [slice(t * LANES, (t + 1) * LANES) for t in range(n_tiles)]

    @pl.when(qb == 0)
    def _():
        for sl in lanes:
            kn_scr[:, sl] = _pair_rmsnorm(k_ref[0, :, sl], kg_ref[:, sl], ones).astype(BF16)
        vb_scr[...] = v_ref[0].astype(BF16)

    m0 = _iota2((1, LANES), 1) < SB_HD
    for t, sl in enumerate(lanes):
        q = _pair_rmsnorm(q_ref[0, :, sl], qg_ref[:, sl], ones) * (SB_HD ** -0.5)
        qs_scr[(2 * t) * tq:(2 * t + 1) * tq, :] = jnp.where(m0, q, 0.0).astype(BF16)
        qs_scr[(2 * t + 1) * tq:(2 * t + 2) * tq, :] = jnp.where(m0, 0.0, q).astype(BF16)
    c_scr[...] = jnp.zeros_like(c_scr)
    acc_scr[...] = jnp.zeros_like(acc_scr)

    tri_after = (_iota2((2 * tk, 2 * tk), 0) > _iota2((2 * tk, 2 * tk), 1)).astype(BF16)

    def tile_pair(j_old, masked):
        start = pl.multiple_of(j_old * tk, tk)
        if masked:
            q_pos = qb * tq + _iota2((tq, 2 * tk), 0)
            before = (start + _iota2((tq, 2 * tk), 1)) < q_pos
        for t, sl in enumerate(lanes):
            kk = kn_scr[pl.ds(start, 2 * tk), sl]
            vv = vb_scr[pl.ds(start, 2 * tk), sl]
            for h in range(2):
                rows = slice((2 * t + h) * tq, (2 * t + h + 1) * tq)
                z = lax.dot_general(qs_scr[rows, :], kk, (((1,), (1,)), ((), ())), preferred_element_type=F32)
                drop = jnp.maximum(z, 0.0) + jnp.log(1.0 + jnp.exp2(jnp.abs(z) * (-LOG2E)))
                if masked:
                    drop = jnp.where(before, drop, 0.0)
                after = jnp.dot(drop.astype(BF16), tri_after, preferred_element_type=F32)
                c = c_scr[rows, :]
                wgt = jnp.exp(z - (drop + after + c))
                if masked:
                    wgt = jnp.where(before, wgt, 0.0)
                c_scr[rows, :] = c + jnp.sum(drop, axis=1, keepdims=True)
                acc_scr[rows, :] += jnp.dot(wgt.astype(BF16), vv, preferred_element_type=F32)

    tiles_per_block = tq // tk
    for i in reversed(range(tiles_per_block // 2)):
        tile_pair(qb * tiles_per_block + 2 * i, True)

    def body(it, carry):
        tile_pair(qb * tiles_per_block - 2 - 2 * it, False)
        return carry

    lax.fori_loop(0, qb * (tiles_per_block // 2), body, 0)
    for t, sl in enumerate(lanes):
        o_ref[0, :, sl] = jnp.where(m0, acc_scr[(2 * t) * tq:(2 * t + 1) * tq, :],
                                    acc_scr[(2 * t + 1) * tq:(2 * t + 2) * tq, :]).astype(o_ref.dtype)


def _sb_attention(proj3, q_gain, k_gain, col0):
    b, s, _ = proj3.shape
    tq = min(SB_TQ, s)
    assert tq % (2 * SB_TK) == 0 and s % tq == 0
    w = SB_LANE_TILES * LANES
    nblk = SB_W // w
    c0 = col0 // SB_LANE_TILES
    gains = lambda g: jnp.tile(g, w // SB_HD).reshape(1, w)
    rows = SB_LANE_TILES * 2 * tq
    return pl.pallas_call(
        _sb_kernel,
        out_shape=jax.ShapeDtypeStruct((b, s, SB_W), BF16),
        grid=(b, nblk, s // tq),
        in_specs=[
            pl.BlockSpec((1, tq, w), lambda i, p, j: (i, j, c0 + p)),
            pl.BlockSpec((1, s, w), lambda i, p, j: (i, 0, c0 + nblk + p), pipeline_mode=pl.Buffered(1)),
            pl.BlockSpec((1, s, w), lambda i, p, j: (i, 0, c0 + 2 * nblk + p), pipeline_mode=pl.Buffered(1)),
            pl.BlockSpec((1, w), lambda i, p, j: (0, 0)),
            pl.BlockSpec((1, w), lambda i, p, j: (0, 0)),
        ],
        out_specs=pl.BlockSpec((1, tq, w), lambda i, p, j: (i, j, p)),
        scratch_shapes=[pltpu.VMEM((s, w), BF16), pltpu.VMEM((s, w), BF16),
                        pltpu.VMEM((rows, LANES), BF16), pltpu.VMEM((rows, 1), F32),
                        pltpu.VMEM((rows, LANES), F32)],
        compiler_params=_cparams(("parallel", "parallel", "arbitrary")),
        name="stick_breaking",
    )(proj3, proj3, proj3, gains(q_gain), gains(k_gain))


def _s5_tables(a_re, a_im, b_re, b_im, c_re, c_im, log_dt, l):
    lam = lax.complex(jnp.minimum(a_re.astype(F32), -1e-4), a_im.astype(F32))
    lam_dt = lam * jnp.exp(log_dt.astype(F32))[:, None]
    lam_bar = jnp.exp(lam_dt)
    b_bar = ((lam_bar - 1.0) / lam)[..., None] * lax.complex(b_re.astype(F32), b_im.astype(F32))
    c = lax.complex(c_re.astype(F32), c_im.astype(F32))
    steps = jnp.arange(l + 1, dtype=F32)
    powers = jnp.exp(lam_dt[:, None, :] * steps[None, :, None])
    g, p = lam.shape
    kern = jnp.einsum('gcp,gjp,gpd->gjcd', c, powers[:, :l], b_bar).real
    lag = jnp.arange(l)[None, :] - jnp.arange(l)[:, None]
    toep = jnp.where((lag >= 0)[None, :, :, None, None], kern[:, jnp.clip(lag, 0, l - 1)], 0.0)
    toep = toep.transpose(0, 1, 4, 2, 3).reshape(g, l * S5_GROUP, l * S5_GROUP)
    b_end = powers[:, :l][:, ::-1, :, None] * b_bar[:, None, :, :]
    b_end = b_end.transpose(0, 1, 3, 2).reshape(g, l * S5_GROUP, p)
    b_end = jnp.concatenate([b_end.real, b_end.imag], axis=-1)
    c_pow = c[:, None, :, :] * powers[:, 1:, None, :]
    c_pow = jnp.concatenate([c_pow.real, -c_pow.imag], axis=-1)
    c_pow = c_pow.transpose(0, 3, 1, 2).reshape(g, 2 * p, l * S5_GROUP)
    lam_l = powers[:, l]
    mul_same = jnp.concatenate([lam_l.real, lam_l.real], axis=-1).reshape(g, 1, 2 * p)
    mul_swap = jnp.concatenate([-lam_l.imag, lam_l.imag], axis=-1).reshape(g, 1, 2 * p)
    return toep.astype(BF16), b_end.astype(BF16), c_pow.astype(BF16), mul_same, mul_swap


def _s5_kernel(u_ref, toep_ref, bend_ref, cpow_ref, same_ref, swap_ref, y_ref, e_scr, x_scr, *, bsz):
    u = u_ref[0]
    e_scr[...] = jnp.dot(u, bend_ref[0], preferred_element_type=F32)
    n_chunks = u.shape[0] // bsz
    same, swap = same_ref[0], swap_ref[0]

    def body(c, x):
        rows = pl.ds(pl.multiple_of(c * bsz, bsz), bsz)
        x_scr[rows, :] = x
        return x * same + pltpu.roll(x, S5_P, axis=1) * swap + e_scr[rows, :]

    lax.fori_loop(0, n_chunks, body, jnp.zeros((bsz, 2 * S5_P), F32))
    y_ref[0] = (jnp.dot(u, toep_ref[0], preferred_element_type=F32)
                + jnp.dot(x_scr[...].astype(BF16), cpow_ref[0], preferred_element_type=F32))


def _s5_ssm(u, tables):
    b, s, _ = u.shape
    l = min(S5_CHUNK, s)
    nc = s // l
    toep, b_end, c_pow, mul_same, mul_swap = tables
    width = l * S5_GROUP
    ug = u.astype(BF16).reshape(b, nc, l, S5_G, S5_GROUP).transpose(3, 1, 0, 2, 4).reshape(S5_G, nc * b, width)
    grp = lambda shape: pl.BlockSpec((1,) + shape, lambda g: (g, 0, 0))
    y = pl.pallas_call(
        functools.partial(_s5_kernel, bsz=b),
        out_shape=jax.ShapeDtypeStruct((S5_G, nc * b, width), F32),
        grid=(S5_G,),
        in_specs=[grp((nc * b, width)), grp((width, width)), grp((width, 2 * S5_P)),
                  grp((2 * S5_P, width)), grp((1, 2 * S5_P)), grp((1, 2 * S5_P))],
        out_specs=grp((nc * b, width)),
        scratch_shapes=[pltpu.VMEM((nc * b, 2 * S5_P), F32), pltpu.VMEM((nc * b, 2 * S5_P), F32)],
        compiler_params=_cparams(("parallel",)),
        name="s5_ssm",
    )(ug, toep, b_end, c_pow, mul_same, mul_swap)
    return y.reshape(S5_G, nc, b, l, S5_GROUP).transpose(2, 1, 3, 0, 4).reshape(b, s, S5_W)


def _s5_post_kernel(y_ref, u_ref, d_ref, w_ref, b_ref, o_ref):
    y = y_ref[...] + d_ref[...] * u_ref[...]
    y = 0.5 * y * (1.0 + jnp.tanh(0.7978845608028654 * (y + 0.044715 * (y * y * y))))
    o_ref[...] = (y * _sigmoid(_dot(y, w_ref[...]) + b_ref[...])).astype(o_ref.dtype)


def _s5_post(y2, proj2, d, w_glu, b_glu):
    n = y2.shape[0]
    tm = min(2048, n)
    vec = pl.BlockSpec((1, S5_W), lambda i: (0, 0))
    return pl.pallas_call(
        _s5_post_kernel,
        out_shape=jax.ShapeDtypeStruct((n, S5_W), BF16),
        grid=(n // tm,),
        in_specs=[pl.BlockSpec((tm, S5_W), lambda i: (i, 0)),
                  pl.BlockSpec((tm, S5_W), lambda i: (i, ODD_IN // S5_W - 1)),
                  vec, pl.BlockSpec((S5_W, S5_W), lambda i: (0, 0)), vec],
        out_specs=pl.BlockSpec((tm, S5_W), lambda i: (i, 0)),
        compiler_params=_cparams(("parallel",)),
        name="s5_glu",
    )(y2, proj2, d.reshape(1, -1), w_glu.astype(BF16), b_glu.reshape(1, -1))


def _hgrn2_kernel(q_ref, f_ref, i_ref, gate_ref, loglb_ref, log1mlb_ref, onemlb_ref, gain_ref,
                  o_ref, st_scr):
    ts = q_ref.shape[1]
    lb = HG_BLOCK
    heads = range(HG_HEADS)
    sls = [slice(h * HG_DK, (h + 1) * HG_DK) for h in heads]

    @pl.when(pl.program_id(1) == 0)
    def _():
        st_scr[...] = jnp.zeros_like(st_scr)

    rr, cc = _iota2((ts, ts), 0), _iota2((ts, ts), 1)
    same = (rr // lb) == (cc // lb)
    causal = same & (cc <= rr)
    tri, blk = causal.astype(BF16), same.astype(BF16)
    sel = jnp.concatenate([jnp.concatenate([tri, tri], axis=1), jnp.concatenate([blk, blk], axis=1)], axis=0)

    fl = f_ref[0]
    hi_arg = loglb_ref[...]
    lo_arg = log1mlb_ref[...] - _softplus(-fl)
    log_f = jnp.maximum(hi_arg, lo_arg) + jnp.log(1.0 + jnp.exp(-jnp.abs(hi_arg - lo_arg)))
    kk = onemlb_ref[...] * _sigmoid(-fl)
    hi, lo = _split(log_f)
    gg = jnp.dot(sel, jnp.concatenate([hi, lo], axis=0), preferred_element_type=F32)
    g_cum, g_tot = gg[:ts], gg[ts:]
    q_in = q_ref[0] * jnp.exp(g_cum)
    k_in = kk * jnp.exp(-g_cum)
    k_end = kk * jnp.exp(g_tot - g_cum)
    decay = jnp.exp(g_tot)
    val = i_ref[0]
    att = [jnp.where(causal, _dot_nt(q_in[:, s], k_in[:, s]), 0.0) for s in sls]
    intra = [_dot(a, val[:, s]) for a, s in zip(att, sls)]
    n_sub = ts // lb
    rows = [slice(c * lb, (c + 1) * lb) for c in range(n_sub)]
    push = [[_dot_tn(val[r, s], k_end[r, s]) for s in sls] for r in rows]

    st = [st_scr[h] for h in heads]
    inter = []
    for c, r in enumerate(rows):
        inter.append([_dot_nt(q_in[r, s], x) for s, x in zip(sls, st)])
        st = [x * decay[c * lb:c * lb + 1, s] + p for x, s, p in zip(st, sls, push[c])]
    for h in heads:
        st_scr[h] = st[h]

    gate = gate_ref[0]
    for h, s in zip(heads, sls):
        o = intra[h] + jnp.concatenate([inter[c][h] for c in range(n_sub)], axis=0)
        o = o * lax.rsqrt(jnp.mean(o * o, axis=-1, keepdims=True) + NORM_EPS) * gain_ref[:, s]
        o_ref[0, :, s] = (o * (gate[:, s] * _sigmoid(gate[:, s]))).astype(o_ref.dtype)


def _hgrn2(proj3, lb, gain):
    b, s, _ = proj3.shape
    ts = min(HG_TS, s)
    w = HG_KW
    lb = lb.astype(F32).reshape(1, w)
    col = lambda c: pl.BlockSpec((1, ts, w), lambda i, j: (i, j, c))
    vec = pl.BlockSpec((1, w), lambda i, j: (0, 0))
    return pl.pallas_call(
        _hgrn2_kernel,
        out_shape=jax.ShapeDtypeStruct((b, s, w), BF16),
        grid=(b, s // ts),
        in_specs=[col(0), col(1), col(2), col(3)] + [vec] * 4,
        out_specs=col(0),
        scratch_shapes=[pltpu.VMEM((HG_HEADS, HG_DK, HG_DK), F32)],
        compiler_params=_cparams(("parallel", "arbitrary")),
        name="hgrn2",
    )(proj3, proj3, proj3, proj3, jnp.log(lb), jnp.log1p(-lb), 1.0 - lb, gain.reshape(1, w))


def _pad_rows(w, rows, start):
    return jnp.zeros((rows, w.shape[1]), BF16).at[start:start + w.shape[0]].set(w.astype(BF16))


def _even_mixer(x2, bsz, seq, norm, w_in, w_out, mu, w0, w2, a0, a2, g2, k_k, k_a, r_k, ln_w, ln_b,
                q_gain, k_gain, v_first, v_mix):
    d = x2.shape[1]
    rw_cols = 3 * RW_W + W_LORA + A_LORA + G_LORA
    pad = RW_PAD - rw_cols
    extra = jnp.zeros((d, pad), w_in.dtype)
    mu_p = jnp.concatenate([mu, jnp.zeros((pad,), mu.dtype)])
    vres = None
    if v_mix is not None:
        v0, v1, v2 = v_mix
        extra = extra.at[:, :V_LORA].set(v1)
        vres = (v0, _pad_rows(v2, RW_SMALL, W_LORA + A_LORA + G_LORA), v_first)
    gap = jnp.zeros((d, SB_COL0 - RW_PAD), w_in.dtype)
    w_pad = jnp.concatenate([w_in[:, :rw_cols], extra, gap, w_in[:, rw_cols:]], axis=1)
    proj3 = _proj(x2, norm, w_pad).reshape(bsz, seq, EVEN_PAD)
    r, k, v, lw, kk, a, g = _rwkv_prep(
        proj3, mu_p, w0, _pad_rows(w2, RW_SMALL, 0), a0, _pad_rows(a2, RW_SMALL, W_LORA),
        _pad_rows(g2, RW_SMALL, W_LORA + A_LORA), k_k, k_a, vres)
    y_rw = _rwkv_chunk(r, k, v, lw, kk, a, g, r_k, ln_w, ln_b)
    y_sb = _sb_attention(proj3, q_gain, k_gain, SB_COL0 // LANES)
    n = bsz * seq
    out = _outproj(x2, y_rw.reshape(n, RW_W), y_sb.reshape(n, SB_W), w_out)
    return out, v


def _odd_mixer(x2, bsz, seq, norm, w_in, w_out, s5_params, d_skip, w_glu, b_glu, lb, hg_gain):
    n = bsz * seq
    w_perm = jnp.concatenate([w_in[:, S5_W:], w_in[:, :S5_W]], axis=1)
    proj2 = _proj(x2, norm, w_perm)
    proj3 = proj2.reshape(bsz, seq, ODD_IN)
    tables = _s5_tables(*s5_params, min(S5_CHUNK, seq))
    ssm = _s5_ssm(proj3[..., ODD_IN - S5_W:], tables)
    y_s5 = _s5_post(ssm.reshape(n, S5_W), proj2, d_skip, w_glu, b_glu)
    y_hg = _hgrn2(proj3, lb, hg_gain)
    return _outproj(x2, y_s5, y_hg.reshape(n, HG_VW), w_out)


def kernel(x, ffn1_norm, ffn1_w13, ffn1_w2, mix_norm, ffn2_norm, ffn2_w13, ffn2_w2, ev_w_in, ev_w_out, rw_mu, rw_w0, rw_w2, rw_a0, rw_a2, rw_g2, rw_k_k, rw_k_a, rw_r_k, rw_ln_w, rw_ln_b, rw_v0, rw_v1, rw_v2, sb_q_gain, sb_k_gain, od_w_in, od_w_out, s5_a_re, s5_a_im, s5_b_re, s5_b_im, s5_c_re, s5_c_im, s5_d, s5_log_dt, s5_w_glu, s5_b_glu, hg_lb, hg_gain):
    bsz, seq, d = x.shape
    depth = ffn1_norm.shape[0]
    lb_all = jnp.cumsum(jax.nn.softmax(hg_lb.astype(F32), axis=0), axis=0)
    lb_all = lb_all - lb_all[0]
    x2 = x.reshape(bsz * seq, d)
    v_first = None
    for layer in range(depth):
        x2 = _ffn(x2, ffn1_norm[layer], ffn1_w13[layer], ffn1_w2[layer])
        if layer % 2 == 0:
            e = layer // 2
            v_mix = None if e == 0 else (rw_v0[e - 1], rw_v1[e - 1], rw_v2[e - 1])
            x2, v = _even_mixer(x2, bsz, seq, mix_norm[layer], ev_w_in[e], ev_w_out[e], rw_mu[e], rw_w0[e],
                                rw_w2[e], rw_a0[e], rw_a2[e], rw_g2[e], rw_k_k[e], rw_k_a[e], rw_r_k[e],
                                rw_ln_w[e], rw_ln_b[e], sb_q_gain[e], sb_k_gain[e], v_first, v_mix)
            if e == 0:
                v_first = v
        else:
            o = layer // 2
            s5_params = (s5_a_re[o], s5_a_im[o], s5_b_re[o], s5_b_im[o], s5_c_re[o], s5_c_im[o], s5_log_dt[o])
            x2 = _odd_mixer(x2, bsz, seq, mix_norm[layer], od_w_in[o], od_w_out[o], s5_params, s5_d[o],
                            s5_w_glu[o], s5_b_glu[o], lb_all[o], hg_gain[o])
        x2 = _ffn(x2, ffn2_norm[layer], ffn2_w13[layer], ffn2_w2[layer])
    return x2.reshape(bsz, seq, d)
```

```python
import functools

import jax
import jax.numpy as jnp
from jax import lax
from jax.experimental import pallas as pl
from jax.experimental.pallas import tpu as pltpu

F32 = jnp.float32
BF16 = jnp.bfloat16

LOG2E = 1.4426950408889634
NORM_EPS = 1e-6
GN_EPS = 64e-5
LANES = 128
RW_HEADS, RW_HD, RW_W = 8, 64, 512
W_LORA, A_LORA, V_LORA, G_LORA = 32, 32, 32, 96
RW_SMALL = 256
RW_PAD = 3 * RW_W + RW_SMALL
SB_W, SB_HD = 512, 64
SB_COL0 = 2048
EVEN_PAD = SB_COL0 + 3 * SB_W
S5_G, S5_GROUP, S5_W, S5_P = 16, 16, 256, 64
HG_HEADS, HG_DK, HG_KW, HG_VW, HG_BLOCK = 6, 128, 768, 768, 16
ODD_IN = S5_W + 2 * HG_KW + 2 * HG_VW
VMEM_LIMIT = 48 * 1024 * 1024

FFN_TM = 512
FFN_CHUNK = 256
RW_CHUNK = 64
RW_TS = 256
S5_CHUNK = 64
SB_TQ = 512
SB_TK = 128
SB_LANE_TILES = 4
HG_TS = 128


def _cparams(sem):
    return pltpu.CompilerParams(dimension_semantics=sem, vmem_limit_bytes=VMEM_LIMIT)


def _dot(a, b):
    return jnp.dot(a.astype(BF16), b.astype(BF16), preferred_element_type=F32)


def _dot_nt(a, b):
    return lax.dot_general(a.astype(BF16), b.astype(BF16), (((1,), (1,)), ((), ())),
                           preferred_element_type=F32)


def _dot_tn(a, b):
    return lax.dot_general(a.astype(BF16), b.astype(BF16), (((0,), (0,)), ((), ())),
                           preferred_element_type=F32)


def _dot_hi(a, b):
    return jnp.dot(a, b, preferred_element_type=F32, precision=lax.Precision.HIGHEST)


def _split(x):
    hi = x.astype(BF16)
    lo = (x - hi.astype(F32)).astype(BF16)
    return hi, lo


def _dot_sel_right(x, e):
    hi, lo = _split(x)
    return (jnp.dot(hi, e, preferred_element_type=F32) + jnp.dot(lo, e, preferred_element_type=F32))


def _dot_sel_left(e, x):
    hi, lo = _split(x)
    return (jnp.dot(e, hi, preferred_element_type=F32) + jnp.dot(e, lo, preferred_element_type=F32))


def _softplus(x):
    return jnp.maximum(x, 0.0) + jnp.log(1.0 + jnp.exp(-jnp.abs(x)))


def _sigmoid(x):
    return jax.nn.sigmoid(x)


def _iota2(shape, dim):
    return lax.broadcasted_iota(jnp.int32, shape, dim)


def _group_ones(n, group):
    return (_iota2((n, n), 0) // group == _iota2((n, n), 1) // group).astype(BF16)


def _rms_rows(x, gain):
    return x * lax.rsqrt(jnp.mean(x * x, axis=-1, keepdims=True) + NORM_EPS) * gain


def _ffn_kernel(x_ref, g_ref, w13_ref, w2_ref, o_ref, h_scr, act_scr):
    dff = w2_ref.shape[0]
    h_scr[...] = _rms_rows(x_ref[...], g_ref[...]).astype(BF16)
    for j in range(dff // FFN_CHUNK):
        lo = j * FFN_CHUNK
        gate = jnp.dot(h_scr[...], w13_ref[:, lo:lo + FFN_CHUNK], preferred_element_type=F32)
        up = jnp.dot(h_scr[...], w13_ref[:, dff + lo:dff + lo + FFN_CHUNK], preferred_element_type=F32)
        act_scr[:, lo:lo + FFN_CHUNK] = (gate * _sigmoid(gate) * up).astype(BF16)
    o_ref[...] = x_ref[...] + 0.5 * jnp.dot(act_scr[...], w2_ref[...], preferred_element_type=F32)


def _ffn(x2, gain, w13, w2):
    n, d = x2.shape
    dff = w2.shape[0]
    tm = min(FFN_TM, n)
    resident = lambda shape: pl.BlockSpec(shape, lambda i: (0, 0), pipeline_mode=pl.Buffered(1))
    return pl.pallas_call(
        _ffn_kernel,
        out_shape=jax.ShapeDtypeStruct((n, d), F32),
        grid=(n // tm,),
        in_specs=[
            pl.BlockSpec((tm, d), lambda i: (i, 0)),
            resident((1, d)),
            resident((d, 2 * dff)),
            resident((dff, d)),
        ],
        out_specs=pl.BlockSpec((tm, d), lambda i: (i, 0)),
        scratch_shapes=[pltpu.VMEM((tm, d), BF16), pltpu.VMEM((tm, dff), BF16)],
        compiler_params=_cparams(("parallel",)),
        name="ffn",
    )(x2, gain.reshape(1, d), w13.astype(BF16), w2.astype(BF16))


def _proj_kernel(x_ref, g_ref, w_ref, o_ref):
    h = _rms_rows(x_ref[...], g_ref[...]).astype(BF16)
    o_ref[...] = jnp.dot(h, w_ref[...], preferred_element_type=F32)


def _proj(x2, gain, w):
    n, d = x2.shape
    c = w.shape[1]
    tm = min(FFN_TM, n)
    return pl.pallas_call(
        _proj_kernel,
        out_shape=jax.ShapeDtypeStruct((n, c), F32),
        grid=(n // tm,),
        in_specs=[
            pl.BlockSpec((tm, d), lambda i: (i, 0)),
            pl.BlockSpec((1, d), lambda i: (0, 0), pipeline_mode=pl.Buffered(1)),
            pl.BlockSpec((d, c), lambda i: (0, 0), pipeline_mode=pl.Buffered(1)),
        ],
        out_specs=pl.BlockSpec((tm, c), lambda i: (i, 0)),
        compiler_params=_cparams(("parallel",)),
        name="mixer_in_proj",
    )(x2, gain.reshape(1, d), w.astype(BF16))


def _outproj_kernel(x_ref, a1_ref, a2_ref, w1_ref, w2_ref, o_ref):
    o_ref[...] = (x_ref[...]
                  + jnp.dot(a1_ref[...], w1_ref[...], preferred_element_type=F32)
                  + jnp.dot(a2_ref[...], w2_ref[...], preferred_element_type=F32))


def _outproj(x2, a1, a2, w_out):
    n, d = x2.shape
    k1, k2 = a1.shape[1], a2.shape[1]
    tm = min(1024, n)
    w = w_out.astype(BF16)
    return pl.pallas_call(
        _outproj_kernel,
        out_shape=jax.ShapeDtypeStruct((n, d), F32),
        grid=(n // tm,),
        in_specs=[
            pl.BlockSpec((tm, d), lambda i: (i, 0)),
            pl.BlockSpec((tm, k1), lambda i: (i, 0)),
            pl.BlockSpec((tm, k2), lambda i: (i, 0)),
            pl.BlockSpec((k1, d), lambda i: (0, 0)),
            pl.BlockSpec((k2, d), lambda i: (0, 0)),
        ],
        out_specs=pl.BlockSpec((tm, d), lambda i: (i, 0)),
        compiler_params=_cparams(("parallel",)),
        name="mixer_out_proj",
    )(x2, a1, a2, w[:k1], w[k1:])


def _rwkv_prep_kernel(*refs, has_vres):
    if has_vres:
        (p_ref, mu_ref, w0_ref, w2_ref, a0_ref, a2_ref, g2_ref, kk_ref, ka_ref, v0_ref, v2_ref, vf_ref,
         r_o, k_o, v_o, lw_o, kk_o, a_o, g_o, carry) = refs
    else:
        (p_ref, mu_ref, w0_ref, w2_ref, a0_ref, a2_ref, g2_ref, kk_ref, ka_ref,
         r_o, k_o, v_o, lw_o, kk_o, a_o, g_o, carry) = refs
    ts = p_ref.shape[1]

    @pl.when(pl.program_id(1) == 0)
    def _():
        carry[...] = jnp.zeros_like(carry)

    x = p_ref[0]
    prev = pltpu.roll(x, 1, axis=0)
    prev = jnp.where(_iota2((ts, 1), 0) == 0, carry[...], prev)
    carry[...] = x[ts - 1:ts, :]
    rw = x + mu_ref[...] * (prev - x)

    r = rw[:, 0:RW_W]
    k = rw[:, RW_W:2 * RW_W]
    v = rw[:, 2 * RW_W:3 * RW_W]
    small = rw[:, 3 * RW_W:RW_PAD]

    log_w = -_softplus(-(w0_ref[...] + _dot(jnp.tanh(small), w2_ref[...]))) - 0.5
    a = _sigmoid(a0_ref[...] + _dot(small, a2_ref[...]))
    g = _dot(_sigmoid(small), g2_ref[...])
    if has_vres:
        v = v + (vf_ref[0] - v) * _sigmoid(v0_ref[...] + _dot(small, v2_ref[...]))

    kk = k * kk_ref[...]
    ss = _dot_sel_right(kk * kk, _group_ones(RW_W, RW_HD))
    kk = kk * lax.rsqrt(jnp.maximum(ss, 1e-24))

    r_o[0] = r
    k_o[0] = k * (1.0 + (a - 1.0) * ka_ref[...])
    v_o[0] = v
    lw_o[0] = -jnp.exp(log_w)
    kk_o[0] = kk
    a_o[0] = a
    g_o[0] = g


def _rwkv_prep(proj3, mu, w0, w2p, a0, a2p, g2p, k_k, k_a, vres):
    b, s, _ = proj3.shape
    ts = min(512, s)
    has_vres = vres is not None
    row = lambda t: t.reshape(1, -1)
    vec = lambda n: pl.BlockSpec((1, n), lambda i, j: (0, 0))
    lora = pl.BlockSpec((RW_SMALL, RW_W), lambda i, j: (0, 0))
    seq = pl.BlockSpec((1, ts, RW_W), lambda i, j: (i, j, 0))
    args = [proj3, row(mu), row(w0), w2p, row(a0), a2p, g2p, row(k_k), row(k_a)]
    specs = [pl.BlockSpec((1, ts, RW_PAD), lambda i, j: (i, j, 0)), vec(RW_PAD), vec(RW_W), lora,
             vec(RW_W), lora, lora, vec(RW_W), vec(RW_W)]
    if has_vres:
        v0, v2p, v_first = vres
        args += [row(v0), v2p, v_first]
        specs += [vec(RW_W), lora, seq]
    out = jax.ShapeDtypeStruct((b, s, RW_W), F32)
    return pl.pallas_call(
        functools.partial(_rwkv_prep_kernel, has_vres=has_vres),
        out_shape=[out] * 7,
        grid=(b, s // ts),
        in_specs=specs,
        out_specs=[seq] * 7,
        scratch_shapes=[pltpu.VMEM((1, RW_PAD), F32)],
        compiler_params=_cparams(("parallel", "arbitrary")),
        name="rwkv_prep",
    )(*args)


def _dot3(a, b):
    ah, al = a if isinstance(a, tuple) else _split(a)
    bh, bl = b if isinstance(b, tuple) else _split(b)
    d = lambda x, y: jnp.dot(x, y, preferred_element_type=F32)
    return d(ah, bh) + (d(ah, bl) + d(al, bh))


def _unit_lower_inverses(ns, l):
    size = ns[0].shape[0]
    row, col = _iota2((size, size), 0), _iota2((size, size), 1)
    eye = (row == col).astype(F32)
    same = lambda blk: (row // blk) == (col // blk)
    n1 = [_split(jnp.where(same(8), n, 0.0)) for n in ns]
    n2 = [_dot3(a, a) for a in n1]
    n2s = [_split(a) for a in n2]
    n4 = [_dot3(a, a) for a in n2s]
    n3 = [_dot3(a, b) for a, b in zip(n1, n2s)]
    p1 = [eye + jnp.where(same(8), n, 0.0) + b + c for n, b, c in zip(ns, n2, n3)]
    t = [p + _dot3(p, q) for p, q in zip(p1, n4)]
    blk = 8
    while blk < l:
        level = same(2 * blk) & jnp.logical_not(same(blk))
        tb = [a.astype(BF16) for a in t]
        tn = [_dot(a, jnp.where(level, n, 0.0)) for a, n in zip(tb, ns)]
        t = [a + _dot(b, c) for a, b, c in zip(t, tn, tb)]
        blk *= 2
    return t


def _rwkv_chunk_kernel(r_ref, k_ref, v_ref, lw_ref, kk_ref, a_ref, g_ref, rk_ref, lnw_ref, lnb_ref,
                       o_ref, h_scr):
    l = RW_CHUNK

    @pl.when(pl.program_id(1) == 0)
    def _():
        h_scr[...] = jnp.zeros_like(h_scr)

    pairs = range(RW_W // LANES)
    n_chunks = r_ref.shape[1] // l
    units = [(c, p) for c in range(n_chunks) for p in pairs]
    rws = [slice(c * l, (c + 1) * l) for c, _ in units]
    sls = [slice(p * LANES, (p + 1) * LANES) for _, p in units]
    tri_incl = (_iota2((l, l), 1) <= _iota2((l, l), 0)).astype(BF16)
    m0 = _iota2((1, LANES), 1) < RW_HD
    rr, cc = _iota2((2 * l, 2 * l), 0), _iota2((2 * l, 2 * l), 1)
    same_head = (rr // l) == (cc // l)
    strict = same_head & ((rr % l) > (cc % l))
    incl = same_head & ((rr % l) >= (cc % l))
    eye = _iota2((LANES, LANES), 0) == _iota2((LANES, LANES), 1)
    head_ones = _group_ones(LANES, RW_HD)
    stack = lambda t: jnp.concatenate([jnp.where(m0, t, 0.0), jnp.where(m0, 0.0, t)], axis=0)
    unstack = lambda t: t[:l] + t[l:]

    load = lambda ref: [ref[0, rw, s] for rw, s in zip(rws, sls)]
    r, k, v, lw, kk_n, a_lr = load(r_ref), load(k_ref), load(v_ref), load(lw_ref), load(kk_ref), load(a_ref)
    gc = [_dot_sel_left(tri_incl, x) for x in lw]
    g_last = [g[l - 1:l, :] for g in gc]
    e_neg = [jnp.exp(-g) for g in gc]
    e_end = [jnp.exp(gl - g) for gl, g in zip(g_last, gc)]
    bv = [x * y for x, y in zip(kk_n, a_lr)]
    a_s = [stack(-x * jnp.exp(g - w)) for x, g, w in zip(kk_n, gc, lw)]
    r_s = [stack(x * jnp.exp(g)) for x, g in zip(r, gc)]
    b_s = [stack(x * e) for x, e in zip(bv, e_neg)]
    k_s = [stack(x * e) for x, e in zip(k, e_neg)]
    v_s = [stack(x) for x in v]
    end_s = [jnp.concatenate([stack(x * e), stack(y * e)], axis=0) for x, y, e in zip(bv, k, e_end)]

    prod = [_dot_nt(jnp.concatenate([a, rq], axis=0), jnp.concatenate([b, kq], axis=0))
            for a, rq, b, kq in zip(a_s, r_s, b_s, k_s)]
    t_inv = _unit_lower_inverses([jnp.where(strict, x[:2 * l, :2 * l], 0.0) for x in prod], l)
    x_s = [_dot(jnp.where(strict, pr[:2 * l, 2 * l:], 0.0), vs) for pr, vs in zip(prod, v_s)]
    wu = [_dot(t, jnp.concatenate([a, x], axis=1)) for t, a, x in zip(t_inv, a_s, x_s)]
    big = [jnp.concatenate([w, jnp.concatenate([jnp.zeros_like(vs), vs], axis=1)], axis=0)
           for w, vs in zip(wu, v_s)]
    r_bk = [jnp.concatenate([jnp.where(incl, pr[2 * l:, :2 * l], 0.0), jnp.where(incl, pr[2 * l:, 2 * l:], 0.0)],
                            axis=1) for pr in prod]
    qy = [_dot(x, b) for x, b in zip(r_bk, big)]
    mn = [_dot_tn(e, b) for e, b in zip(end_s, big)]

    q_s = [rq + x[:, :LANES] for rq, x in zip(r_s, qy)]
    m_mat = [x[:, :LANES] + jnp.where(eye, jnp.broadcast_to(jnp.exp(gl), (LANES, LANES)), 0.0)
             for x, gl in zip(mn, g_last)]
    state = [h_scr[p] for p in pairs]
    y = []
    for c in range(n_chunks):
        u0 = c * len(pairs)
        y += [unstack(_dot(q_s[u0 + p], state[p]) + qy[u0 + p][:, LANES:]) for p in pairs]
        state = [_dot3(m_mat[u0 + p], state[p]) + mn[u0 + p][:, LANES:] for p in pairs]
    for p in pairs:
        h_scr[p] = state[p]

    for u, (rw, sl) in enumerate(zip(rws, sls)):
        mean = _dot_sel_right(y[u], head_ones) * (1.0 / RW_HD)
        yc = y[u] - mean
        var = _dot_sel_right(yc * yc, head_ones) * (1.0 / RW_HD)
        yn = yc * lax.rsqrt(var + GN_EPS) * lnw_ref[:, sl] + lnb_ref[:, sl]
        bonus = _dot_sel_right(r[u] * k[u] * rk_ref[:, sl], head_ones) * v[u]
        o_ref[0, rw, sl] = ((yn + bonus) * g_ref[0, rw, sl]).astype(o_ref.dtype)


def _rwkv_chunk(r, k, v, lw, kk, a, g, r_k, ln_w, ln_b):
    b, s, _ = r.shape
    ts = min(RW_TS, s)
    seq = pl.BlockSpec((1, ts, RW_W), lambda i, j: (i, j, 0))
    vec = pl.BlockSpec((1, RW_W), lambda i, j: (0, 0))
    return pl.pallas_call(
        _rwkv_chunk_kernel,
        out_shape=jax.ShapeDtypeStruct((b, s, RW_W), BF16),
        grid=(b, s // ts),
        in_specs=[seq] * 7 + [vec] * 3,
        out_specs=seq,
        scratch_shapes=[pltpu.VMEM((RW_W // LANES, LANES, LANES), F32)],
        compiler_params=_cparams(("parallel", "arbitrary")),
        name="rwkv_chunk",
    )(r, k, v, lw, kk, a, g, r_k.reshape(1, -1), ln_w.reshape(1, -1), ln_b.reshape(1, -1))


def _pair_rmsnorm(t, gain_row, ones):
    ms = _dot_sel_right(t * t, ones) * (1.0 / SB_HD)
    return t * lax.rsqrt(ms + NORM_EPS) * gain_row


def _sb_kernel(q_ref, k_ref, v_ref, qg_ref, kg_ref, o_ref, kn_scr, vb_scr, qs_scr, c_scr, acc_scr):
    tq = q_ref.shape[1]
    tk = SB_TK
    n_tiles = q_ref.shape[2] // LANES
    qb = pl.program_id(2)
    ones = _group_ones(LANES, SB_HD)
    lanes = [slice(t * LANES, (t + 1) * LANES) for t in range(n_tiles)]

    @pl.when(qb == 0)
    def _():
        for sl in lanes:
            kn_scr[:, sl] = _pair_rmsnorm(k_ref[0, :, sl], kg_ref[:, sl], ones).astype(BF16)
        vb_scr[...] = v_ref[0].astype(BF16)

    m0 = _iota2((1, LANES), 1) < SB_HD
    for t, sl in enumerate(lanes):
        q = _pair_rmsnorm(q_ref[0, :, sl], qg_ref[:, sl], ones) * (SB_HD ** -0.5)
        qs_scr[(2 * t) * tq:(2 * t + 1) * tq, :] = jnp.where(m0, q, 0.0).astype(BF16)
        qs_scr[(2 * t + 1) * tq:(2 * t + 2) * tq, :] = jnp.where(m0, 0.0, q).astype(BF16)
    c_scr[...] = jnp.zeros_like(c_scr)
    acc_scr[...] = jnp.zeros_like(acc_scr)

    tri_after = (_iota2((2 * tk, 2 * tk), 0) > _iota2((2 * tk, 2 * tk), 1)).astype(BF16)

    def tile_pair(j_old, masked, row0=0):
        start = pl.multiple_of(j_old * tk, tk)
        nq = tq - row0
        if masked:
            q_pos = qb * tq + row0 + _iota2((nq, 2 * tk), 0)
            before = (start + _iota2((nq, 2 * tk), 1)) < q_pos
        for t, sl in enumerate(lanes):
            kk = kn_scr[pl.ds(start, 2 * tk), sl]
            vv = vb_scr[pl.ds(start, 2 * tk), sl]
            for h in range(2):
                rows = slice((2 * t + h) * tq + row0, (2 * t + h + 1) * tq)
                z = lax.dot_general(qs_scr[rows, :], kk, (((1,), (1,)), ((), ())), preferred_element_type=F32)
                drop = jnp.maximum(z, 0.0) + jnp.log(1.0 + jnp.exp2(jnp.abs(z) * (-LOG2E)))
                if masked:
                    drop = jnp.where(before, drop, 0.0)
                after = jnp.dot(drop.astype(BF16), tri_after, preferred_element_type=F32)
                c = c_scr[rows, :]
                wgt = jnp.exp(z - (drop + after + c))
                if masked:
                    wgt = jnp.where(before, wgt, 0.0)
                c_scr[rows, :] = c + jnp.sum(drop, axis=1, keepdims=True)
                acc_scr[rows, :] += jnp.dot(wgt.astype(BF16), vv, preferred_element_type=F32)

    tiles_per_block = tq // tk
    for i in reversed(range(tiles_per_block // 2)):
        tile_pair(qb * tiles_per_block + 2 * i, True, row0=2 * i * tk)

    def body(it, carry):
        tile_pair(qb * tiles_per_block - 2 - 2 * it, False)
        return carry

    lax.fori_loop(0, qb * (tiles_per_block // 2), body, 0)
    for t, sl in enumerate(lanes):
        o_ref[0, :, sl] = jnp.where(m0, acc_scr[(2 * t) * tq:(2 * t + 1) * tq, :],
                                    acc_scr[(2 * t + 1) * tq:(2 * t + 2) * tq, :]).astype(o_ref.dtype)


def _sb_attention(proj3, q_gain, k_gain, col0):
    b, s, _ = proj3.shape
    tq = min(SB_TQ, s)
    assert tq % (2 * SB_TK) == 0 and s % tq == 0
    w = SB_LANE_TILES * LANES
    nblk = SB_W // w
    c0 = col0 // SB_LANE_TILES
    gains = lambda g: jnp.tile(g, w // SB_HD).reshape(1, w)
    rows = SB_LANE_TILES * 2 * tq
    return pl.pallas_call(
        _sb_kernel,
        out_shape=jax.ShapeDtypeStruct((b, s, SB_W), BF16),
        grid=(b, nblk, s // tq),
        in_specs=[
            pl.BlockSpec((1, tq, w), lambda i, p, j: (i, j, c0 + p)),
            pl.BlockSpec((1, s, w), lambda i, p, j: (i, 0, c0 + nblk + p), pipeline_mode=pl.Buffered(1)),
            pl.BlockSpec((1, s, w), lambda i, p, j: (i, 0, c0 + 2 * nblk + p), pipeline_mode=pl.Buffered(1)),
            pl.BlockSpec((1, w), lambda i, p, j: (0, 0)),
            pl.BlockSpec((1, w), lambda i, p, j: (0, 0)),
        ],
        out_specs=pl.BlockSpec((1, tq, w), lambda i, p, j: (i, j, p)),
        scratch_shapes=[pltpu.VMEM((s, w), BF16), pltpu.VMEM((s, w), BF16),
                        pltpu.VMEM((rows, LANES), BF16), pltpu.VMEM((rows, 1), F32),
                        pltpu.VMEM((rows, LANES), F32)],
        compiler_params=_cparams(("parallel", "parallel", "arbitrary")),
        name="stick_breaking",
    )(proj3, proj3, proj3, gains(q_gain), gains(k_gain))


def _s5_tables(a_re, a_im, b_re, b_im, c_re, c_im, log_dt, l):
    lam = lax.complex(jnp.minimum(a_re.astype(F32), -1e-4), a_im.astype(F32))
    lam_dt = lam * jnp.exp(log_dt.astype(F32))[:, None]
    lam_bar = jnp.exp(lam_dt)
    b_bar = ((lam_bar - 1.0) / lam)[..., None] * lax.complex(b_re.astype(F32), b_im.astype(F32))
    c = lax.complex(c_re.astype(F32), c_im.astype(F32))
    steps = jnp.arange(l + 1, dtype=F32)
    powers = jnp.exp(lam_dt[:, None, :] * steps[None, :, None])
    g, p = lam.shape
    kern = jnp.einsum('gcp,gjp,gpd->gjcd', c, powers[:, :l], b_bar).real
    lag = jnp.arange(l)[None, :] - jnp.arange(l)[:, None]
    toep = jnp.where((lag >= 0)[None, :, :, None, None], kern[:, jnp.clip(lag, 0, l - 1)], 0.0)
    toep = toep.transpose(0, 1, 4, 2, 3).reshape(g, l * S5_GROUP, l * S5_GROUP)
    b_end = powers[:, :l][:, ::-1, :, None] * b_bar[:, None, :, :]
    b_end = b_end.transpose(0, 1, 3, 2).reshape(g, l * S5_GROUP, p)
    b_end = jnp.concatenate([b_end.real, b_end.imag], axis=-1)
    c_pow = c[:, None, :, :] * powers[:, 1:, None, :]
    c_pow = jnp.concatenate([c_pow.real, -c_pow.imag], axis=-1)
    c_pow = c_pow.transpose(0, 3, 1, 2).reshape(g, 2 * p, l * S5_GROUP)
    lam_l = powers[:, l]
    mul_same = jnp.concatenate([lam_l.real, lam_l.real], axis=-1).reshape(g, 1, 2 * p)
    mul_swap = jnp.concatenate([-lam_l.imag, lam_l.imag], axis=-1).reshape(g, 1, 2 * p)
    return toep.astype(BF16), b_end.astype(BF16), c_pow.astype(BF16), mul_same, mul_swap


def _s5_kernel(u_ref, toep_ref, bend_ref, cpow_ref, same_ref, swap_ref, y_ref, e_scr, x_scr, *, bsz):
    u = u_ref[0]
    e_scr[...] = jnp.dot(u, bend_ref[0], preferred_element_type=F32)
    n_chunks = u.shape[0] // bsz
    same, swap = same_ref[0], swap_ref[0]

    def body(c, x):
        rows = pl.ds(pl.multiple_of(c * bsz, bsz), bsz)
        x_scr[rows, :] = x
        return x * same + pltpu.roll(x, S5_P, axis=1) * swap + e_scr[rows, :]

    lax.fori_loop(0, n_chunks, body, jnp.zeros((bsz, 2 * S5_P), F32))
    y_ref[0] = (jnp.dot(u, toep_ref[0], preferred_element_type=F32)
                + jnp.dot(x_scr[...].astype(BF16), cpow_ref[0], preferred_element_type=F32))


def _s5_ssm(u, tables):
    b, s, _ = u.shape
    l = min(S5_CHUNK, s)
    nc = s // l
    toep, b_end, c_pow, mul_same, mul_swap = tables
    width = l * S5_GROUP
    ug = u.astype(BF16).reshape(b, nc, l, S5_G, S5_GROUP).transpose(3, 1, 0, 2, 4).reshape(S5_G, nc * b, width)
    grp = lambda shape: pl.BlockSpec((1,) + shape, lambda g: (g, 0, 0))
    y = pl.pallas_call(
        functools.partial(_s5_kernel, bsz=b),
        out_shape=jax.ShapeDtypeStruct((S5_G, nc * b, width), F32),
        grid=(S5_G,),
        in_specs=[grp((nc * b, width)), grp((width, width)), grp((width, 2 * S5_P)),
                  grp((2 * S5_P, width)), grp((1, 2 * S5_P)), grp((1, 2 * S5_P))],
        out_specs=grp((nc * b, width)),
        scratch_shapes=[pltpu.VMEM((nc * b, 2 * S5_P), F32), pltpu.VMEM((nc * b, 2 * S5_P), F32)],
        compiler_params=_cparams(("parallel",)),
        name="s5_ssm",
    )(ug, toep, b_end, c_pow, mul_same, mul_swap)
    return y.reshape(S5_G, nc, b, l, S5_GROUP).transpose(2, 1, 3, 0, 4).reshape(b, s, S5_W)


def _s5_post_kernel(y_ref, u_ref, d_ref, w_ref, b_ref, o_ref):
    y = y_ref[...] + d_ref[...] * u_ref[...]
    y = 0.5 * y * (1.0 + jnp.tanh(0.7978845608028654 * (y + 0.044715 * (y * y * y))))
    o_ref[...] = (y * _sigmoid(_dot(y, w_ref[...]) + b_ref[...])).astype(o_ref.dtype)


def _s5_post(y2, proj2, d, w_glu, b_glu):
    n = y2.shape[0]
    tm = min(2048, n)
    vec = pl.BlockSpec((1, S5_W), lambda i: (0, 0))
    return pl.pallas_call(
        _s5_post_kernel,
        out_shape=jax.ShapeDtypeStruct((n, S5_W), BF16),
        grid=(n // tm,),
        in_specs=[pl.BlockSpec((tm, S5_W), lambda i: (i, 0)),
                  pl.BlockSpec((tm, S5_W), lambda i: (i, ODD_IN // S5_W - 1)),
                  vec, pl.BlockSpec((S5_W, S5_W), lambda i: (0, 0)), vec],
        out_specs=pl.BlockSpec((tm, S5_W), lambda i: (i, 0)),
        compiler_params=_cparams(("parallel",)),
        name="s5_glu",
    )(y2, proj2, d.reshape(1, -1), w_glu.astype(BF16), b_glu.reshape(1, -1))


def _hgrn2_kernel(q_ref, f_ref, i_ref, gate_ref, loglb_ref, log1mlb_ref, onemlb_ref, gain_ref,
                  o_ref, st_scr):
    ts = q_ref.shape[1]
    lb = HG_BLOCK
    heads = range(HG_HEADS)
    sls = [slice(h * HG_DK, (h + 1) * HG_DK) for h in heads]

    @pl.when(pl.program_id(1) == 0)
    def _():
        st_scr[...] = jnp.zeros_like(st_scr)

    rr, cc = _iota2((ts, ts), 0), _iota2((ts, ts), 1)
    same = (rr // lb) == (cc // lb)
    causal = same & (cc <= rr)
    tri, blk = causal.astype(BF16), same.astype(BF16)
    sel = jnp.concatenate([jnp.concatenate([tri, tri], axis=1), jnp.concatenate([blk, blk], axis=1)], axis=0)

    fl = f_ref[0]
    hi_arg = loglb_ref[...]
    lo_arg = log1mlb_ref[...] - _softplus(-fl)
    log_f = jnp.maximum(hi_arg, lo_arg) + jnp.log(1.0 + jnp.exp(-jnp.abs(hi_arg - lo_arg)))
    kk = onemlb_ref[...] * _sigmoid(-fl)
    hi, lo = _split(log_f)
    gg = jnp.dot(sel, jnp.concatenate([hi, lo], axis=0), preferred_element_type=F32)
    g_cum, g_tot = gg[:ts], gg[ts:]
    q_in = q_ref[0] * jnp.exp(g_cum)
    k_in = kk * jnp.exp(-g_cum)
    k_end = kk * jnp.exp(g_tot - g_cum)
    decay = jnp.exp(g_tot)
    val = i_ref[0]
    att = [jnp.where(causal, _dot_nt(q_in[:, s], k_in[:, s]), 0.0) for s in sls]
    intra = [_dot(a, val[:, s]) for a, s in zip(att, sls)]
    n_sub = ts // lb
    rows = [slice(c * lb, (c + 1) * lb) for c in range(n_sub)]
    push = [[_dot_tn(val[r, s], k_end[r, s]) for s in sls] for r in rows]

    st = [st_scr[h] for h in heads]
    inter = []
    for c, r in enumerate(rows):
        inter.append([_dot_nt(q_in[r, s], x) for s, x in zip(sls, st)])
        st = [x * decay[c * lb:c * lb + 1, s] + p for x, s, p in zip(st, sls, push[c])]
    for h in heads:
        st_scr[h] = st[h]

    gate = gate_ref[0]
    for h, s in zip(heads, sls):
        o = intra[h] + jnp.concatenate([inter[c][h] for c in range(n_sub)], axis=0)
        o = o * lax.rsqrt(jnp.mean(o * o, axis=-1, keepdims=True) + NORM_EPS) * gain_ref[:, s]
        o_ref[0, :, s] = (o * (gate[:, s] * _sigmoid(gate[:, s]))).astype(o_ref.dtype)


def _hgrn2(proj3, lb, gain):
    b, s, _ = proj3.shape
    ts = min(HG_TS, s)
    w = HG_KW
    lb = lb.astype(F32).reshape(1, w)
    col = lambda c: pl.BlockSpec((1, ts, w), lambda i, j: (i, j, c))
    vec = pl.BlockSpec((1, w), lambda i, j: (0, 0))
    return pl.pallas_call(
        _hgrn2_kernel,
        out_shape=jax.ShapeDtypeStruct((b, s, w), BF16),
        grid=(b, s // ts),
        in_specs=[col(0), col(1), col(2), col(3)] + [vec] * 4,
        out_specs=col(0),
        scratch_shapes=[pltpu.VMEM((HG_HEADS, HG_DK, HG_DK), F32)],
        compiler_params=_cparams(("parallel", "arbitrary")),
        name="hgrn2",
    )(proj3, proj3, proj3, proj3, jnp.log(lb), jnp.log1p(-lb), 1.0 - lb, gain.reshape(1, w))


def _pad_rows(w, rows, start):
    return jnp.zeros((rows, w.shape[1]), BF16).at[start:start + w.shape[0]].set(w.astype(BF16))


def _even_mixer(x2, bsz, seq, norm, w_in, w_out, mu, w0, w2, a0, a2, g2, k_k, k_a, r_k, ln_w, ln_b,
                q_gain, k_gain, v_first, v_mix):
    d = x2.shape[1]
    rw_cols = 3 * RW_W + W_LORA + A_LORA + G_LORA
    pad = RW_PAD - rw_cols
    extra = jnp.zeros((d, pad), w_in.dtype)
    mu_p = jnp.concatenate([mu, jnp.zeros((pad,), mu.dtype)])
    vres = None
    if v_mix is not None:
        v0, v1, v2 = v_mix
        extra = extra.at[:, :V_LORA].set(v1)
        vres = (v0, _pad_rows(v2, RW_SMALL, W_LORA + A_LORA + G_LORA), v_first)
    gap = jnp.zeros((d, SB_COL0 - RW_PAD), w_in.dtype)
    w_pad = jnp.concatenate([w_in[:, :rw_cols], extra, gap, w_in[:, rw_cols:]], axis=1)
    proj3 = _proj(x2, norm, w_pad).reshape(bsz, seq, EVEN_PAD)
    r, k, v, lw, kk, a, g = _rwkv_prep(
        proj3, mu_p, w0, _pad_rows(w2, RW_SMALL, 0), a0, _pad_rows(a2, RW_SMALL, W_LORA),
        _pad_rows(g2, RW_SMALL, W_LORA + A_LORA), k_k, k_a, vres)
    y_rw = _rwkv_chunk(r, k, v, lw, kk, a, g, r_k, ln_w, ln_b)
    y_sb = _sb_attention(proj3, q_gain, k_gain, SB_COL0 // LANES)
    n = bsz * seq
    out = _outproj(x2, y_rw.reshape(n, RW_W), y_sb.reshape(n, SB_W), w_out)
    return out, v


def _odd_mixer(x2, bsz, seq, norm, w_in, w_out, s5_params, d_skip, w_glu, b_glu, lb, hg_gain):
    n = bsz * seq
    w_perm = jnp.concatenate([w_in[:, S5_W:], w_in[:, :S5_W]], axis=1)
    proj2 = _proj(x2, norm, w_perm)
    proj3 = proj2.reshape(bsz, seq, ODD_IN)
    tables = _s5_tables(*s5_params, min(S5_CHUNK, seq))
    ssm = _s5_ssm(proj3[..., ODD_IN - S5_W:], tables)
    y_s5 = _s5_post(ssm.reshape(n, S5_W), proj2, d_skip, w_glu, b_glu)
    y_hg = _hgrn2(proj3, lb, hg_gain)
    return _outproj(x2, y_s5, y_hg.reshape(n, HG_VW), w_out)


def kernel(x, ffn1_norm, ffn1_w13, ffn1_w2, mix_norm, ffn2_norm, ffn2_w13, ffn2_w2, ev_w_in, ev_w_out, rw_mu, rw_w0, rw_w2, rw_a0, rw_a2, rw_g2, rw_k_k, rw_k_a, rw_r_k, rw_ln_w, rw_ln_b, rw_v0, rw_v1, rw_v2, sb_q_gain, sb_k_gain, od_w_in, od_w_out, s5_a_re, s5_a_im, s5_b_re, s5_b_im, s5_c_re, s5_c_im, s5_d, s5_log_dt, s5_w_glu, s5_b_glu, hg_lb, hg_gain):
    bsz, seq, d = x.shape
    depth = ffn1_norm.shape[0]
    lb_all = jnp.cumsum(jax.nn.softmax(hg_lb.astype(F32), axis=0), axis=0)
    lb_all = lb_all - lb_all[0]
    x2 = x.reshape(bsz * seq, d)
    v_first = None
    for layer in range(depth):
        x2 = _ffn(x2, ffn1_norm[layer], ffn1_w13[layer], ffn1_w2[layer])
        if layer % 2 == 0:
            e = layer // 2
            v_mix = None if e == 0 else (rw_v0[e - 1], rw_v1[e - 1], rw_v2[e - 1])
            x2, v = _even_mixer(x2, bsz, seq, mix_norm[layer], ev_w_in[e], ev_w_out[e], rw_mu[e], rw_w0[e],
                                rw_w2[e], rw_a0[e], rw_a2[e], rw_g2[e], rw_k_k[e], rw_k_a[e], rw_r_k[e],
                                rw_ln_w[e], rw_ln_b[e], sb_q_gain[e], sb_k_gain[e], v_first, v_mix)
            if e == 0:
                v_first = v
        else:
            o = layer // 2
            s5_params = (s5_a_re[o], s5_a_im[o], s5_b_re[o], s5_b_im[o], s5_c_re[o], s5_c_im[o], s5_log_dt[o])
            x2 = _odd_mixer(x2, bsz, seq, mix_norm[layer], od_w_in[o], od_w_out[o], s5_params, s5_d[o],
                            s5_w_glu[o], s5_b_glu[o], lb_all[o], hg_gain[o])
        x2 = _ffn(x2, ffn2_norm[layer], ffn2_w13[layer], ffn2_w2[layer])
    return x2.reshape(bsz, seq, d)
```
